```python
import jax, jax.numpy as jnp
from jax import lax
import numpy as np

D_MODEL = 2048
BATCH = 2
SEQ = 4096
DEPTH = 4
DEC_BATCH = 8
DEC_SEQ = 8
PAST_LEN = 16384
PAGE_SIZE = 128

HEAD_DIM = 128
SB_WIDTH = D_MODEL // 2
SB_HEADS = SB_WIDTH // HEAD_DIM
SB_BIAS_INIT = -6.0
GMLP_WIDTH = D_MODEL // 4
GMLP_HEADS = 4
GMLP_CH = GMLP_WIDTH // GMLP_HEADS
CHUNK = 128
POOL_WIDTH = D_MODEL - SB_WIDTH - GMLP_WIDTH
POOL_WINDOWS = (2, 4, 8, 16)
POOL_GROUPS = len(POOL_WINDOWS)
POOL_CH = POOL_WIDTH // POOL_GROUPS
POOL_STATE = max(POOL_WINDOWS) - 1
MIX_WIDTH = SB_WIDTH + GMLP_WIDTH + POOL_WIDTH
D_IN = 3 * SB_WIDTH + 2 * GMLP_WIDTH + POOL_WIDTH
D_FF = ((8 * D_MODEL // 3 + 127) // 128) * 128
CONV_WIDTH = 3
PLE_DIM = 256
Q_BLOCK = 128
EPS = 1e-6

kernel_name = "hymba_stickbreak_gmlp_pool_step"


def rms_norm(x, w):
    xf = x.astype(jnp.float32)
    y = xf * lax.rsqrt(jnp.mean(xf * xf, axis=-1, keepdims=True) + EPS)
    return (y * w.astype(jnp.float32)).astype(x.dtype)


def stick_breaking_attention(q, k, v, sb_bias, q_pos, k_pos):
    B, Tq, H, D = q.shape
    qb = Q_BLOCK if Tq % Q_BLOCK == 0 else Tq
    nb = Tq // qb
    scale = D ** -0.5
    q_blocks = q.reshape(B, nb, qb, H, D).transpose(1, 0, 2, 3, 4)
    pos_blocks = q_pos.reshape(nb, qb)
    bias = sb_bias.astype(jnp.float32)[None, :, None, None]

    def one_block(args):
        q_blk, p_blk = args
        z = jnp.einsum('bqhd,bkhd->bhqk', q_blk, k,
                       preferred_element_type=jnp.float32) * scale + bias
        visible = (k_pos[None, :] < p_blk[:, None])[None, None]
        log_beta = jax.nn.log_sigmoid(z)
        log_keep = jnp.where(visible, jax.nn.log_sigmoid(-z), 0.0)
        after = lax.cumsum(log_keep, axis=3, reverse=True) - log_keep
        w = jnp.where(visible, jnp.exp(log_beta + after), 0.0)
        return jnp.einsum('bhqk,bkhd->bqhd', w.astype(v.dtype), v)

    out = lax.map(one_block, (q_blocks, pos_blocks))
    return out.transpose(1, 0, 2, 3, 4).reshape(B, Tq, H, D)


def chunk_gmlp(u, v, ws, bias):
    B, T, _ = v.shape
    n_chunks = -(-T // CHUNK)
    pad = n_chunks * CHUNK - T
    vc = jnp.pad(v, ((0, 0), (0, pad), (0, 0))).reshape(B, n_chunks, CHUNK, GMLP_HEADS, GMLP_CH)
    causal = jnp.tril(jnp.ones((CHUNK, CHUNK), dtype=bool))
    ws_masked = jnp.where(causal[None], ws, 0.0)
    mixed = jnp.einsum('gts,bnsgc->bntgc', ws_masked, vc) + bias.T[None, None, :, :, None]
    mixed = mixed.reshape(B, n_chunks * CHUNK, GMLP_WIDTH)[:, :T]
    return u * mixed


def multi_scale_pool(xp, prev, pos0, pool_w, pool_scale):
    B, T, _ = xp.shape
    P = prev.shape[1]
    ext = jnp.concatenate([prev, xp], axis=1).astype(jnp.float32)
    cs = jnp.concatenate([jnp.zeros((B, 1, POOL_WIDTH), jnp.float32),
                          jnp.cumsum(ext, axis=1)], axis=1)
    end = P + 1 + jnp.arange(T)
    pos = pos0 + jnp.arange(T)
    cs_end = cs[:, end]
    groups = []
    for g, w in enumerate(POOL_WINDOWS):
        sl = slice(g * POOL_CH, (g + 1) * POOL_CH)
        start = jnp.maximum(end - w, 0)
        count = jnp.minimum(pos + 1, w).astype(jnp.float32)
        mean = (cs_end[..., sl] - cs[:, start, sl]) / count[None, :, None]
        groups.append(mean - ext[:, P:, sl])
    pooled = jnp.stack(groups, axis=2).astype(xp.dtype)
    mapped = jnp.einsum('btgc,gcd->btgd', pooled, pool_w).reshape(B, T, POOL_WIDTH)
    return mapped * pool_scale, ext[:, -POOL_STATE:].astype(xp.dtype)


def causal_dwconv(h, prev, w, b):
    T = h.shape[1]
    ext = jnp.concatenate([prev, h], axis=1)
    out = b
    for i in range(CONV_WIDTH):
        out = out + w[i] * ext[:, i:i + T]
    return out, ext[:, -(CONV_WIDTH - 1):]


def decoder_layer(x, p, past_k, past_v, pool_prev, conv_prev,
                  attn_norm_w, w_in, sb_bias, gmlp_ws, gmlp_b, pool_w, pool_scale, w_out,
                  ffn_norm_w, w_up, conv_w, conv_b, w_down, ple_norm_w, w_ple, w_ple_gate):
    B, T, _ = x.shape
    past = past_k.shape[1]
    h = rms_norm(x, attn_norm_w)
    proj = h @ w_in
    splits = [SB_WIDTH, 2 * SB_WIDTH, 3 * SB_WIDTH,
              3 * SB_WIDTH + GMLP_WIDTH, 3 * SB_WIDTH + 2 * GMLP_WIDTH]
    q, k, v, gu, gv, xp = jnp.split(proj, splits, axis=-1)
    q = q.reshape(B, T, SB_HEADS, HEAD_DIM)
    k = k.reshape(B, T, SB_HEADS, HEAD_DIM)
    v = v.reshape(B, T, SB_HEADS, HEAD_DIM)
    k_all = jnp.concatenate([past_k, k], axis=1)
    v_all = jnp.concatenate([past_v, v], axis=1)
    q_pos = past + jnp.arange(T)
    k_pos = jnp.arange(past + T)
    o_sb = stick_breaking_attention(q, k_all, v_all, sb_bias, q_pos, k_pos).reshape(B, T, SB_WIDTH)
    gu = jax.nn.gelu(gu)
    gv = jax.nn.gelu(gv)
    o_gm = chunk_gmlp(gu, gv, gmlp_ws, gmlp_b)
    o_pool, pool_state = multi_scale_pool(xp, pool_prev, past, pool_w, pool_scale)
    x = x + jnp.concatenate([o_sb, o_gm, o_pool], axis=-1) @ w_out
    up = rms_norm(x, ffn_norm_w) @ w_up
    up, conv_state = causal_dwconv(up, conv_prev, conv_w, conv_b)
    a, g = jnp.split(up, 2, axis=-1)
    x = x + (jax.nn.silu(g) * a) @ w_down
    gate = jax.nn.sigmoid(rms_norm(x, ple_norm_w) @ w_ple_gate)
    x = x + (p @ w_ple) * gate
    return x, k, v, pool_state, conv_state, gv


def setup_inputs(seed: int = 0) -> dict:
    key = jax.random.key(seed)
    ks = jax.random.split(key, 26)
    f32 = jnp.float32
    n_pages = PAST_LEN // PAGE_SIZE
    used = DEC_BATCH * n_pages
    n_phys = used + max(1, used // 4)

    def nrm(k, shape, scale):
        return jax.random.normal(k, shape, f32) * scale

    page_table = jax.random.permutation(ks[6], n_phys)[:used].reshape(DEC_BATCH, n_pages).astype(jnp.int32)
    return {
        "x_prompt": nrm(ks[0], (BATCH, SEQ, D_MODEL), 1.0),
        "x_sample": nrm(ks[1], (DEC_BATCH, DEC_SEQ, D_MODEL), 1.0),
        "cache_k": nrm(ks[2], (DEPTH, n_phys, PAGE_SIZE, SB_HEADS, HEAD_DIM), 1.0),
        "cache_v": nrm(ks[3], (DEPTH, n_phys, PAGE_SIZE, SB_HEADS, HEAD_DIM), 1.0),
        "state_pool": nrm(ks[4], (DEPTH, DEC_BATCH, POOL_STATE, POOL_WIDTH), 1.0),
        "state_conv": nrm(ks[5], (DEPTH, DEC_BATCH, CONV_WIDTH - 1, 2 * D_FF), 1.0),
        "page_table": page_table,
        "p_prompt": nrm(ks[7], (DEPTH, BATCH, SEQ, PLE_DIM), 1.0),
        "p_sample": nrm(ks[8], (DEPTH, DEC_BATCH, DEC_SEQ, PLE_DIM), 1.0),
        "attn_norm_w": 1.0 + nrm(ks[9], (DEPTH, D_MODEL), 0.02),
        "w_in": nrm(ks[10], (DEPTH, D_MODEL, D_IN), D_MODEL ** -0.5),
        "sb_bias": SB_BIAS_INIT + nrm(ks[25], (DEPTH, SB_HEADS), 0.1),
        "gmlp_ws": nrm(ks[11], (DEPTH, GMLP_HEADS, CHUNK, CHUNK), CHUNK ** -0.5),
        "gmlp_b": 1.0 + nrm(ks[12], (DEPTH, GMLP_HEADS, CHUNK), 0.02),
        "pool_w": nrm(ks[13], (DEPTH, POOL_GROUPS, POOL_CH, POOL_CH), POOL_CH ** -0.5),
        "pool_scale": 1.0 + nrm(ks[14], (DEPTH, POOL_WIDTH), 0.1),
        "w_out": nrm(ks[15], (DEPTH, MIX_WIDTH, D_MODEL), MIX_WIDTH ** -0.5),
        "ffn_norm_w": 1.0 + nrm(ks[16], (DEPTH, D_MODEL), 0.02),
        "w_up": nrm(ks[17], (DEPTH, D_MODEL, 2 * D_FF), D_MODEL ** -0.5),
        "conv_w": nrm(ks[18], (DEPTH, CONV_WIDTH, 2 * D_FF), CONV_WIDTH ** -0.5),
        "conv_b": nrm(ks[19], (DEPTH, 2 * D_FF), 0.02),
        "w_down": nrm(ks[20], (DEPTH, D_FF, D_MODEL), D_FF ** -0.5),
        "ple_norm_w": 1.0 + nrm(ks[21], (DEPTH, D_MODEL), 0.02),
        "w_ple": nrm(ks[22], (DEPTH, PLE_DIM, D_MODEL), PLE_DIM ** -0.5),
        "w_ple_gate": nrm(ks[23], (DEPTH, D_MODEL, D_MODEL), D_MODEL ** -0.5),
        "final_norm_w": 1.0 + nrm(ks[24], (D_MODEL,), 0.02),
    }


def reference(x_prompt, x_sample, cache_k, cache_v, state_pool, state_conv, page_table,
              p_prompt, p_sample, attn_norm_w, w_in, sb_bias, gmlp_ws, gmlp_b, pool_w, pool_scale,
              w_out, ffn_norm_w, w_up, conv_w, conv_b, w_down, ple_norm_w, w_ple,
              w_ple_gate, final_norm_w):
    bp = x_prompt.shape[0]
    bs, n_pages = page_table.shape
    xp_, xs_ = x_prompt, x_sample
    kp_l, vp_l, ks_l, vs_l = [], [], [], []
    poolp_l, pools_l, convp_l, convs_l, gvs_l = [], [], [], [], []
    for l in range(DEPTH):
        lw = (attn_norm_w[l], w_in[l], sb_bias[l], gmlp_ws[l], gmlp_b[l], pool_w[l], pool_scale[l],
              w_out[l], ffn_norm_w[l], w_up[l], conv_w[l], conv_b[l], w_down[l],
              ple_norm_w[l], w_ple[l], w_ple_gate[l])
        empty_kv = jnp.zeros((bp, 0, SB_HEADS, HEAD_DIM), x_prompt.dtype)
        empty_pool = jnp.zeros((bp, 0, POOL_WIDTH), x_prompt.dtype)
        zero_conv = jnp.zeros((bp, CONV_WIDTH - 1, 2 * D_FF), x_prompt.dtype)
        xp_, kp, vp, poolp, convp, _ = decoder_layer(
            xp_, p_prompt[l], empty_kv, empty_kv, empty_pool, zero_conv, *lw)
        past_k = cache_k[l][page_table].reshape(bs, n_pages * PAGE_SIZE, SB_HEADS, HEAD_DIM)
        past_v = cache_v[l][page_table].reshape(bs, n_pages * PAGE_SIZE, SB_HEADS, HEAD_DIM)
        xs_, ksn, vsn, pools, convs, gvs = decoder_layer(
            xs_, p_sample[l], past_k, past_v, state_pool[l], state_conv[l], *lw)
        kp_l.append(kp); vp_l.append(vp); ks_l.append(ksn); vs_l.append(vsn)
        poolp_l.append(poolp); pools_l.append(pools)
        convp_l.append(convp); convs_l.append(convs); gvs_l.append(gvs)
    y_prompt = rms_norm(xp_, final_norm_w)
    y_sample = rms_norm(xs_, final_norm_w)
    k_prompt = jnp.stack(kp_l)
    v_prompt = jnp.stack(vp_l)
    k_sample = jnp.stack(ks_l)
    v_sample = jnp.stack(vs_l)
    pool_prompt = jnp.stack(poolp_l)
    pool_sample = jnp.stack(pools_l)
    conv_prompt = jnp.stack(convp_l)
    conv_sample = jnp.stack(convs_l)
    gmlp_v_sample = jnp.stack(gvs_l)
    return (y_prompt, y_sample, k_prompt, v_prompt, k_sample, v_sample,
            pool_prompt, pool_sample, conv_prompt, conv_sample, gmlp_v_sample)
```

```python
import functools

import jax
import jax.numpy as jnp
from jax import lax
from jax.experimental import pallas as pl
from jax.experimental.pallas import tpu as pltpu

F32 = jnp.float32
BF16 = jnp.bfloat16

EPS = 1e-6
HEAD_DIM = 128
SB_HEADS = 8
SB_WIDTH = SB_HEADS * HEAD_DIM
GMLP_HEADS = 4
CHUNK = 128
GMLP_WIDTH = GMLP_HEADS * CHUNK
POOL_WINDOWS = (2, 4, 8, 16)
POOL_CH = 128
POOL_WIDTH = POOL_CH * len(POOL_WINDOWS)
POOL_STATE = max(POOL_WINDOWS) - 1
POOL_HALO = 16
CONV_WIDTH = 3
CONV_HALO = 8
REST_WIDTH = 2 * GMLP_WIDTH + POOL_WIDTH
SCALE = HEAD_DIM ** -0.5

IN_TN = 512
FF_TN = 512
CAST_ROWS = 256
V7X_VMEM_LIMIT = 56 * 1024 * 1024


def _params(sem):
    return pltpu.CompilerParams(dimension_semantics=sem, vmem_limit_bytes=V7X_VMEM_LIMIT)


def _rms_bf16(x, w):
    ms = jnp.mean(x * x, axis=-1, keepdims=True)
    return ((x * lax.rsqrt(ms + EPS)) * w).astype(BF16)


def _cast_kernel(x_ref, o_ref, *, rows_valid):
    x = x_ref[...]
    tr, c = x.shape
    if rows_valid is not None:
        r = pl.program_id(1) * tr + lax.broadcasted_iota(jnp.int32, (tr, 1), 0)
        x = jnp.where(r < rows_valid, x, 0.0)
    o_ref[:, 0:c] = x.astype(BF16)
    if o_ref.shape[1] > c:
        o_ref[:, c:] = jnp.zeros((tr, o_ref.shape[1] - c), BF16)


def _to_bf16(w, tr, n_split=1, out_rows=None, out_cols=None):
    n_l, r, c_all = w.shape
    c = c_all // n_split
    out_rows = out_rows or r
    out_cols = out_cols or c
    assert c_all == n_split * c and out_rows % tr == 0 and out_rows - r < tr and out_cols >= c
    out = pl.pallas_call(
        functools.partial(_cast_kernel, rows_valid=r if out_rows > r else None),
        grid=(n_l, out_rows // tr, n_split),
        in_specs=[pl.BlockSpec((None, tr, c), lambda l, i, s: (l, i, s))],
        out_specs=pl.BlockSpec((None, None, tr, out_cols), lambda l, i, s: (l, s, i, 0)),
        out_shape=jax.ShapeDtypeStruct((n_l, n_split, out_rows, out_cols), BF16),
        compiler_params=_params(("arbitrary", "arbitrary", "arbitrary")),
        name="to_bf16",
    )(w)
    return out if n_split > 1 else out.reshape(n_l, out_rows, out_cols)


def _inproj_kernel(x_ref, nw_ref, w_ref, *refs, stacked_kv):
    if stacked_kv:
        k_ref, v_ref, qkvb_ref, rest_ref, xn_ref = refs[2:]
    else:
        kv_ref, qkvb_ref, rest_ref, xn_ref = refs
    j = pl.program_id(1)
    tm = x_ref.shape[0]
    heads_per_tile = IN_TN // HEAD_DIM

    @pl.when(j == 0)
    def _():
        xn_ref[...] = _rms_bf16(x_ref[...], nw_ref[...])

    acc = jnp.dot(xn_ref[...], w_ref[...], preferred_element_type=F32)

    @pl.when(j < 6)
    def _():
        qkvb_ref[0] = acc.astype(BF16)

    if stacked_kv:
        for first_tile, ref in ((2, k_ref), (4, v_ref)):
            for c in range(SB_HEADS // heads_per_tile):
                @pl.when(j == first_tile + c)
                def _(ref=ref, c=c):
                    for hh in range(heads_per_tile):
                        ref[pl.ds(c * heads_per_tile + hh, tm, stride=SB_HEADS), :] = (
                            acc[:, hh * HEAD_DIM:(hh + 1) * HEAD_DIM])
    else:
        @pl.when((j >= 2) & (j < 6))
        def _():
            kv_ref[0] = acc

    @pl.when((j == 6) | (j == 7))
    def _():
        rest_ref[...] = jax.nn.gelu(acc)

    @pl.when(j == 8)
    def _():
        rest_ref[...] = acc


def _inproj(x2, nw, w_in, layer, tm, kv_stack=None):
    r, d = x2.shape
    n_j = w_in.shape[2] // IN_TN
    assert 3 * SB_WIDTH == 6 * IN_TN and n_j == 9 and r % tm == 0

    def kv_map(i, j):
        c = jnp.clip(j - 2, 0, 3)
        return (c // 2, i, c % 2)

    def qkv_map(i, j):
        c = jnp.minimum(j, 5)
        return (c // 2, i, c % 2)

    in_specs = [
        pl.BlockSpec((tm, d), lambda i, j: (i, 0)),
        pl.BlockSpec((None, 1, d), lambda i, j: (layer, 0, 0)),
        pl.BlockSpec((None, d, IN_TN), lambda i, j: (layer, 0, j)),
    ]
    common_specs = [
        pl.BlockSpec((1, tm, IN_TN), qkv_map),
        pl.BlockSpec((tm, IN_TN), lambda i, j: (i, jnp.maximum(j - 6, 0))),
    ]
    common_shapes = [
        jax.ShapeDtypeStruct((3, r, SB_WIDTH), BF16),
        jax.ShapeDtypeStruct((r, REST_WIDTH), F32),
    ]
    if kv_stack is None:
        operands, aliases = (x2, nw, w_in), {}
        kv_specs = [pl.BlockSpec((1, tm, IN_TN), kv_map)]
        kv_shapes = [jax.ShapeDtypeStruct((2, r, SB_WIDTH), F32)]
    else:
        operands, aliases = (x2, nw, w_in) + tuple(kv_stack), {3: 0, 4: 1}
        in_specs += [pl.BlockSpec(memory_space=pl.ANY)] * 2
        kv_specs = [pl.BlockSpec((None, tm * SB_HEADS, HEAD_DIM), lambda i, j: (layer, i, 0))] * 2
        kv_shapes = [jax.ShapeDtypeStruct(a.shape, a.dtype) for a in kv_stack]
    return pl.pallas_call(
        functools.partial(_inproj_kernel, stacked_kv=kv_stack is not None),
        grid=(r // tm, n_j),
        in_specs=in_specs,
        out_specs=kv_specs + common_specs,
        out_shape=kv_shapes + common_shapes,
        input_output_aliases=aliases,
        scratch_shapes=[pltpu.VMEM((tm, d), BF16)],
        compiler_params=_params(("arbitrary", "arbitrary")),
        name="inproj",
    )(*operands)


def _sb_terms(s, nbias):
    nz = s * (-SCALE) + nbias
    l = jnp.log(1.0 + jnp.exp(-jnp.abs(nz)))
    lk = jnp.minimum(nz, 0.0) - l
    lb = lk - nz
    return lb, lk


def _suffix_sum(lk, u):
    hi = lk.astype(BF16)
    lo = (lk - hi.astype(F32)).astype(BF16)
    return (jnp.dot(hi, u, preferred_element_type=F32)
            + jnp.dot(lo, u, preferred_element_type=F32))


def _later_key_matrix(n):
    row = lax.broadcasted_iota(jnp.int32, (n, n), 0)
    col = lax.broadcasted_iota(jnp.int32, (n, n), 1)
    return row, col, jnp.where(row > col, 1.0, 0.0).astype(BF16)


def _nt_dot(a, b):
    return lax.dot_general(a, b, (((1,), (1,)), ((), ())), preferred_element_type=F32)


def _attn_prompt_kernel(bias_ref, q_ref, k_ref, v_ref, o_ref, *, blk, layer):
    qi = pl.program_id(1)
    row, col, u = _later_key_matrix(blk)
    vis = col < row
    heads = [slice(h * HEAD_DIM, (h + 1) * HEAD_DIM) for h in range(SB_HEADS)]
    nbias = [-bias_ref[layer, h] for h in range(SB_HEADS)]

    def block(kb, state, masked):
        keys = pl.ds(pl.multiple_of(kb * blk, blk), blk)
        s = [_nt_dot(q_ref[0, 0, :, sl], k_ref[0, 0, keys, sl]) for sl in heads]
        terms = [_sb_terms(s[h], nbias[h]) for h in range(SB_HEADS)]
        if masked:
            terms = [(lb, jnp.where(vis, lk, 0.0)) for lb, lk in terms]
        after = [_suffix_sum(lk, u) for _, lk in terms]
        w = [jnp.exp(terms[h][0] + after[h] + state[h][0]) for h in range(SB_HEADS)]
        if masked:
            w = [jnp.where(vis, wh, 0.0) for wh in w]
        new_state = []
        for h, sl in enumerate(heads):
            carry, acc = state[h]
            acc = acc + jnp.dot(w[h].astype(BF16), v_ref[0, 0, keys, sl], preferred_element_type=F32)
            carry = carry + after[h][:, 0:1] + terms[h][1][:, 0:1]
            new_state.append((carry, acc))
        return tuple(new_state)

    state = tuple((jnp.zeros((blk, 1), F32), jnp.zeros((blk, HEAD_DIM), F32)) for _ in heads)
    state = block(qi, state, True)
    state = lax.fori_loop(0, qi, lambda it, st: block(qi - 1 - it, st, False), state)
    for h, sl in enumerate(heads):
        o_ref[0, :, sl] = state[h][1].astype(o_ref.dtype)


def _attn_prompt(qkv4, sb_bias, layer, blk):
    _, b, t, _ = qkv4.shape
    assert t % blk == 0
    return pl.pallas_call(
        functools.partial(_attn_prompt_kernel, blk=blk, layer=layer),
        grid=(b, t // blk),
        in_specs=[
            pl.BlockSpec(memory_space=pltpu.SMEM),
            pl.BlockSpec((1, 1, blk, SB_WIDTH), lambda bi, qi: (0, bi, qi, 0)),
            pl.BlockSpec((1, 1, t, SB_WIDTH), lambda bi, qi: (1, bi, 0, 0)),
            pl.BlockSpec((1, 1, t, SB_WIDTH), lambda bi, qi: (2, bi, 0, 0)),
        ],
        out_specs=pl.BlockSpec((1, blk, SB_WIDTH), lambda bi, qi: (bi, qi, 0)),
        out_shape=jax.ShapeDtypeStruct((b, t, SB_WIDTH), BF16),
        compiler_params=_params(("arbitrary", "arbitrary")),
        name="attn_prompt",
    )(sb_bias, qkv4, qkv4, qkv4)


def _attn_sample_kernel(pt_ref, q_ref, kn_ref, vn_ref, nb_ref, *refs, n_pp, t_new):
    k_refs = refs[:n_pp]
    v_refs = refs[n_pp:2 * n_pp]
    o_ref = refs[2 * n_pp]
    qbd_ref, acc_ref, carry_ref = refs[2 * n_pp + 1:]
    step = pl.program_id(1)
    n_rows = SB_HEADS * t_new
    page = k_refs[0].shape[0] // SB_HEADS
    _, _, u = _later_key_matrix(page)
    nbias = nb_ref[...]

    def blocks(ks, vs, vis):
        qbd = qbd_ref[...]
        terms = [_sb_terms(_nt_dot(qbd, k), nbias) for k in ks]
        if vis is not None:
            terms = [(lb, jnp.where(vis, lk, 0.0)) for lb, lk in terms]
        after = [_suffix_sum(lk, u) for _, lk in terms]
        carry = carry_ref[...]
        acc = acc_ref[...]
        for (lb, lk), af, v in zip(terms, after, vs):
            w = jnp.exp(lb + af + carry)
            if vis is not None:
                w = jnp.where(vis, w, 0.0)
            acc = acc + jnp.dot(w.astype(BF16), v, preferred_element_type=F32)
            carry = carry + af[:, 0:1] + lk[:, 0:1]
        carry_ref[...] = carry
        acc_ref[...] = acc

    @pl.when(step == 0)
    def _():
        q = q_ref[0].astype(F32)
        qt = jnp.concatenate([q] * SB_HEADS, axis=0)
        rh = lax.broadcasted_iota(jnp.int32, (n_rows, SB_WIDTH), 0) // t_new
        ch = lax.broadcasted_iota(jnp.int32, (n_rows, SB_WIDTH), 1) // HEAD_DIM
        qbd_ref[...] = jnp.where(rh == ch, qt, 0.0).astype(BF16)
        acc_ref[...] = jnp.zeros_like(acc_ref)
        carry_ref[...] = jnp.zeros_like(carry_ref)
        pad = jnp.zeros((page - t_new, SB_WIDTH), F32)
        kn = jnp.concatenate([kn_ref[0].astype(F32), pad], axis=0).astype(BF16)
        vn = jnp.concatenate([vn_ref[0].astype(F32), pad], axis=0).astype(BF16)
        rt = lax.broadcasted_iota(jnp.int32, (n_rows, page), 0) % t_new
        ck = lax.broadcasted_iota(jnp.int32, (n_rows, page), 1)
        blocks([kn], [vn], ck < rt)

    def heads_on_lanes(ref):
        return jnp.concatenate(
            [ref[pl.ds(h, page, stride=SB_HEADS), :].astype(BF16) for h in range(SB_HEADS)], axis=1)

    blocks([heads_on_lanes(r) for r in k_refs], [heads_on_lanes(r) for r in v_refs], None)

    @pl.when(step == pl.num_programs(1) - 1)
    def _():
        for h in range(SB_HEADS):
            o_ref[0, :, h * HEAD_DIM:(h + 1) * HEAD_DIM] = acc_ref[
                h * t_new:(h + 1) * t_new, h * HEAD_DIM:(h + 1) * HEAD_DIM].astype(o_ref.dtype)


def _attn_sample(q3, kn3, vn3, nbias_col, cache_k, cache_v, page_table, layer, n_pp):
    b, t_new, _ = q3.shape
    n_pages = page_table.shape[1]
    page = cache_k.shape[2] // SB_HEADS
    assert n_pages % n_pp == 0 and t_new <= page
    n_rows = SB_HEADS * t_new

    def kv_spec(p):
        return pl.BlockSpec(
            (None, None, page * SB_HEADS, HEAD_DIM),
            lambda bi, s, pt: (layer, pt[bi, n_pages - 1 - (s * n_pp + p)], 0, 0))

    row_spec = pl.BlockSpec((1, t_new, SB_WIDTH), lambda bi, s, pt: (bi, 0, 0))
    grid_spec = pltpu.PrefetchScalarGridSpec(
        num_scalar_prefetch=1,
        grid=(b, n_pages // n_pp),
        in_specs=[row_spec, row_spec, row_spec,
                  pl.BlockSpec((n_rows, 1), lambda bi, s, pt: (0, 0))]
        + [kv_spec(p) for p in range(n_pp)] + [kv_spec(p) for p in range(n_pp)],
        out_specs=row_spec,
        scratch_shapes=[pltpu.VMEM((n_rows, SB_WIDTH), BF16),
                        pltpu.VMEM((n_rows, SB_WIDTH), F32),
                        pltpu.VMEM((n_rows, 1), F32)],
    )
    return pl.pallas_call(
        functools.partial(_attn_sample_kernel, n_pp=n_pp, t_new=t_new),
        grid_spec=grid_spec,
        out_shape=jax.ShapeDtypeStruct((b, t_new, SB_WIDTH), BF16),
        compiler_params=_params(("arbitrary", "arbitrary")),
        name="attn_sample",
    )(page_table, q3, kn3, vn3, nbias_col, *([cache_k] * n_pp), *([cache_v] * n_pp))


def _mix_kernel(rest_ref, prev_ref, ws_ref, gbt_ref, pw_ref, ps_ref, o_ref, pbuf_ref, *, nb, tr, pos0):
    ti = pl.program_id(1)

    @pl.when(ti == 0)
    def _():
        pbuf_ref[:, 0:POOL_HALO] = prev_ref[...]

    @pl.when(ti > 0)
    def _():
        pbuf_ref[:, 0:POOL_HALO] = pbuf_ref[:, tr:tr + POOL_HALO]

    pbuf_ref[:, POOL_HALO:] = rest_ref[:, :, 2 * GMLP_WIDTH:]

    tc = min(tr, CHUNK)
    row, col, _ = _later_key_matrix(CHUNK)
    causal = row >= col
    pos = (pos0 + ti * tr + lax.broadcasted_iota(jnp.int32, (tr, 1), 0) + 1).astype(F32)

    wsm_heads = [jnp.where(causal, ws_ref[g], 0.0).astype(BF16) for g in range(GMLP_HEADS)]

    for bb in range(nb):
        for g in range(GMLP_HEADS):
            sl = slice(g * CHUNK, (g + 1) * CHUNK)
            wsm = wsm_heads[g]
            bias = gbt_ref[0:tc, g:g + 1]
            for c in range(tr // tc):
                rs = slice(c * tc, (c + 1) * tc)
                u = rest_ref[bb, rs, sl]
                v = rest_ref[bb, rs, GMLP_WIDTH + g * CHUNK:GMLP_WIDTH + (g + 1) * CHUNK]
                if tc < CHUNK:
                    v = jnp.concatenate([v, jnp.zeros((CHUNK - tc, CHUNK), F32)], axis=0)
                mixed = jnp.dot(wsm, v.astype(BF16), preferred_element_type=F32)[0:tc] + bias
                o_ref[bb, rs, sl] = (u * mixed).astype(o_ref.dtype)
        for g, win in enumerate(POOL_WINDOWS):
            sl = slice(g * POOL_CH, (g + 1) * POOL_CH)
            x = pbuf_ref[bb, POOL_HALO:POOL_HALO + tr, sl]
            tot = x
            for d in range(1, win):
                tot = tot + pbuf_ref[bb, POOL_HALO - d:POOL_HALO - d + tr, sl]
            pooled = tot / jnp.minimum(pos, float(win)) - x
            mapped = jnp.dot(pooled.astype(BF16), pw_ref[g].astype(BF16), preferred_element_type=F32)
            o_ref[bb, :, GMLP_WIDTH + g * POOL_CH:GMLP_WIDTH + (g + 1) * POOL_CH] = (
                mapped * ps_ref[:, sl]).astype(o_ref.dtype)


def _mix(rest3, pool_prev, gmlp_ws, gmlp_bt, pool_w, pool_scale, layer, nb, tr, pos0):
    b, t, _ = rest3.shape
    assert b % nb == 0 and t % tr == 0 and (tr % CHUNK == 0 or (tr < CHUNK and t == tr))
    const = lambda *shape: pl.BlockSpec((None,) + shape, lambda bi, ti: (layer,) + (0,) * len(shape))
    return pl.pallas_call(
        functools.partial(_mix_kernel, nb=nb, tr=tr, pos0=pos0),
        grid=(b // nb, t // tr),
        in_specs=[
            pl.BlockSpec((nb, tr, REST_WIDTH), lambda bi, ti: (bi, ti, 0)),
            pl.BlockSpec((nb, POOL_HALO, POOL_WIDTH), lambda bi, ti: (bi, 0, 0)),
            const(GMLP_HEADS, CHUNK, CHUNK),
            const(CHUNK, GMLP_HEADS),
            const(len(POOL_WINDOWS), POOL_CH, POOL_CH),
            const(1, POOL_WIDTH),
        ],
        out_specs=pl.BlockSpec((nb, tr, GMLP_WIDTH + POOL_WIDTH), lambda bi, ti: (bi, ti, 0)),
        out_shape=jax.ShapeDtypeStruct((b, t, GMLP_WIDTH + POOL_WIDTH), BF16),
        scratch_shapes=[pltpu.VMEM((nb, POOL_HALO + tr, POOL_WIDTH), F32)],
        compiler_params=_params(("arbitrary", "arbitrary")),
        name="mix",
    )(rest3, pool_prev, gmlp_ws, gmlp_bt, pool_w, pool_scale)


def _outproj_kernel(x_ref, sb_ref, mix_ref, w_ref, o_ref):
    o_ref[...] = (x_ref[...]
                  + jnp.dot(sb_ref[...], w_ref[0:SB_WIDTH, :], preferred_element_type=F32)
                  + jnp.dot(mix_ref[...], w_ref[SB_WIDTH:, :], preferred_element_type=F32))


def _outproj(x2, o_sb, o_mix, w_out, layer, tm):
    r, d = x2.shape
    return pl.pallas_call(
        _outproj_kernel,
        grid=(r // tm,),
        in_specs=[
            pl.BlockSpec((tm, d), lambda i: (i, 0)),
            pl.BlockSpec((tm, SB_WIDTH), lambda i: (i, 0)),
            pl.BlockSpec((tm, GMLP_WIDTH + POOL_WIDTH), lambda i: (i, 0)),
            pl.BlockSpec((None,) + w_out.shape[1:], lambda i: (layer, 0, 0)),
        ],
        out_specs=pl.BlockSpec((tm, d), lambda i: (i, 0)),
        out_shape=jax.ShapeDtypeStruct((r, d), F32),
        compiler_params=_params(("arbitrary",)),
        name="outproj",
    )(x2, o_sb, o_mix, w_out)


def _ffn_kernel(x_ref, nw_ref, wa_ref, wg_ref, cwa_ref, cwg_ref, cba_ref, cbg_ref, wd_ref,
                pa_ref, pg_ref, y_ref, csa_ref, csg_ref,
                xn_ref, acc_ref, bufa_ref, bufg_ref, haloa_ref, halog_ref, *, nb, tr):
    bi = pl.program_id(0)
    ti = pl.program_id(1)
    j = pl.program_id(2)
    n_j = pl.num_programs(2) - 1
    d = x_ref.shape[-1]
    tn = wa_ref.shape[-1]
    halves = ((cwa_ref, cba_ref, pa_ref, csa_ref, bufa_ref, haloa_ref),
              (cwg_ref, cbg_ref, pg_ref, csg_ref, bufg_ref, halog_ref))

    def up_tile():
        xn = xn_ref[...]
        return tuple(jnp.dot(xn, w_ref[...], preferred_element_type=F32).reshape(nb, tr, tn)
                     for w_ref in (wa_ref, wg_ref))

    def park(ups):
        bufa_ref[:, CONV_HALO:] = ups[0]
        bufg_ref[:, CONV_HALO:] = ups[1]

    def down_tile():
        jj = j - 1
        outs = []
        for cw_ref, cb_ref, prev_ref, cs_ref, buf_ref, halo_ref in halves:
            buf_ref[:, 0:CONV_HALO] = jnp.where(ti == 0, prev_ref[...], halo_ref[jj])
            last = buf_ref[:, tr:tr + CONV_HALO]
            halo_ref[jj] = last
            cs_ref[:, 0] = last
            out = cb_ref[...]
            for i in range(CONV_WIDTH):
                lo = CONV_HALO - (CONV_WIDTH - 1) + i
                out = out + cw_ref[i:i + 1, :] * buf_ref[:, lo:lo + tr]
            outs.append(out.reshape(nb * tr, tn))
        a, g = outs
        h = (jax.nn.silu(g) * a).astype(BF16)
        return jnp.dot(h, wd_ref[...], preferred_element_type=F32)

    @pl.when((bi == 0) & (ti == 0) & (j == 0))
    def _():
        haloa_ref[...] = jnp.zeros_like(haloa_ref)
        halog_ref[...] = jnp.zeros_like(halog_ref)

    @pl.when(j == 0)
    def _():
        xn_ref[...] = _rms_bf16(x_ref[...].reshape(nb * tr, d), nw_ref[...])
        park(up_tile())

    @pl.when(j == 1)
    def _():
        ups = up_tile()
        acc_ref[...] = down_tile()
        park(ups)

    @pl.when((j > 1) & (j < n_j))
    def _():
        ups = up_tile()
        acc_ref[...] += down_tile()
        park(ups)

    @pl.when(j == n_j)
    def _():
        y_ref[...] = x_ref[...] + (acc_ref[...] + down_tile()).reshape(nb, tr, d)


def _ffn(x3, nw, w_up, conv_w, conv_b, w_down, prev_a, prev_g, layer, nb, tr):
    b, t, d = x3.shape
    npad = w_up.shape[3]
    n_j = npad // FF_TN
    n_t = t // tr
    assert b % nb == 0 and t % tr == 0 and tr >= CONV_HALO and npad % FF_TN == 0 and n_j >= 2
    up_j = lambda j: jnp.minimum(j, n_j - 1)
    dn_j = lambda j: jnp.maximum(j - 1, 0)
    up = lambda half: pl.BlockSpec((None, None, d, FF_TN), lambda bi, ti, j: (layer, half, 0, up_j(j)))
    col = lambda rows, half: pl.BlockSpec((None, None, rows, FF_TN), lambda bi, ti, j: (layer, half, 0, dn_j(j)))
    halo = pl.BlockSpec((nb, CONV_HALO, FF_TN), lambda bi, ti, j: (bi, 0, dn_j(j)))
    last = pl.BlockSpec((nb, 1, CONV_HALO, FF_TN), lambda bi, ti, j: (bi, ti, 0, dn_j(j)))
    xs = pl.BlockSpec((nb, tr, d), lambda bi, ti, j: (bi, ti, 0))
    return pl.pallas_call(
        functools.partial(_ffn_kernel, nb=nb, tr=tr),
        grid=(b // nb, n_t, n_j + 1),
        in_specs=[
            xs,
            pl.BlockSpec((None, 1, d), lambda bi, ti, j: (layer, 0, 0)),
            up(0), up(1),
            col(CONV_WIDTH, 0), col(CONV_WIDTH, 1), col(1, 0), col(1, 1),
            pl.BlockSpec((None, FF_TN, d), lambda bi, ti, j: (layer, dn_j(j), 0)),
            halo, halo,
        ],
        out_specs=[xs, last, last],
        out_shape=[
            jax.ShapeDtypeStruct((b, t, d), F32),
            jax.ShapeDtypeStruct((b, n_t, CONV_HALO, npad), F32),
            jax.ShapeDtypeStruct((b, n_t, CONV_HALO, npad), F32),
        ],
        scratch_shapes=[
            pltpu.VMEM((nb * tr, d), BF16),
            pltpu.VMEM((nb * tr, d), F32),
            pltpu.VMEM((nb, CONV_HALO + tr, FF_TN), F32),
            pltpu.VMEM((nb, CONV_HALO + tr, FF_TN), F32),
            pltpu.VMEM((n_j, nb, CONV_HALO, FF_TN), F32),
            pltpu.VMEM((n_j, nb, CONV_HALO, FF_TN), F32),
        ],
        compiler_params=_params(("arbitrary", "arbitrary", "arbitrary")),
        name="ffn",
    )(x3, nw, w_up, w_up, conv_w, conv_w, conv_b, conv_b, w_down, prev_a, prev_g)


def _ple_kernel(x_ref, p_ref, nw_ref, wg_ref, wp_ref, o_ref):
    x = x_ref[...]
    gate = jax.nn.sigmoid(jnp.dot(_rms_bf16(x, nw_ref[...]), wg_ref[...], preferred_element_type=F32))
    emb = jnp.dot(p_ref[...].astype(BF16), wp_ref[...], preferred_element_type=F32)
    o_ref[...] = x + emb * gate


def _ple(x2, p, nw, w_gate, w_ple, layer, tm):
    r, d = x2.shape
    whole = lambda a: pl.BlockSpec((None,) + a.shape[1:], lambda i: (layer, 0, 0))
    return pl.pallas_call(
        _ple_kernel,
        grid=(r // tm,),
        in_specs=[
            pl.BlockSpec((tm, d), lambda i: (i, 0)),
            pl.BlockSpec((None, tm, p.shape[2]), lambda i: (layer, i, 0)),
            whole(nw), whole(w_gate), whole(w_ple),
        ],
        out_specs=pl.BlockSpec((tm, d), lambda i: (i, 0)),
        out_shape=jax.ShapeDtypeStruct((r, d), F32),
        compiler_params=_params(("arbitrary",)),
        name="ple",
    )(x2, p, nw, w_gate, w_ple)


def _norm_kernel(x_ref, nw_ref, o_ref):
    x = x_ref[...]
    ms = jnp.mean(x * x, axis=-1, keepdims=True)
    o_ref[...] = (x * lax.rsqrt(ms + EPS)) * nw_ref[...]


def _final_norm(x2, nw, tm):
    r, d = x2.shape
    return pl.pallas_call(
        _norm_kernel,
        grid=(r // tm,),
        in_specs=[pl.BlockSpec((tm, d), lambda i: (i, 0)), pl.BlockSpec((1, d), lambda i: (0, 0))],
        out_specs=pl.BlockSpec((tm, d), lambda i: (i, 0)),
        out_shape=jax.ShapeDtypeStruct((r, d), F32),
        compiler_params=_params(("arbitrary",)),
        name="final_norm",
    )(x2, nw)


def _tile(n, cap):
    t = min(n, cap)
    while n % t:
        t //= 2
    return t


def _layer(x3, p, w, layer, pool_prev, conv_prev_a, conv_prev_g, pos0, attn_fn, kv_stack=None):
    b, t, d = x3.shape
    r = b * t
    one_tile = t < CHUNK
    nb, tr = (b, t) if one_tile else (1, _tile(t, 512))
    tm = r if one_tile else _tile(r, 1024)

    *kv, qkv, rest = _inproj(x3.reshape(r, d), w["attn_norm_w"], w["w_in"], layer, tm, kv_stack)
    o_sb = attn_fn(qkv.reshape(3, b, t, SB_WIDTH))
    o_mix = _mix(rest.reshape(b, t, REST_WIDTH), pool_prev, w["gmlp_ws"], w["gmlp_bt"],
                 w["pool_w"], w["pool_scale"], layer, nb, tr, pos0)
    tm_res = r if one_tile else _tile(r, 256)
    x2 = _outproj(x3.reshape(r, d), o_sb.reshape(r, SB_WIDTH), o_mix.reshape(r, -1), w["w_out"], layer, tm_res)
    x3, cs_a, cs_g = _ffn(x2.reshape(b, t, d), w["ffn_norm_w"], w["w_up"], w["conv_w"], w["conv_b"], w["w_down"],
                          conv_prev_a, conv_prev_g, layer, nb, tr)
    x2 = _ple(x3.reshape(r, d), p, w["ple_norm_w"], w["w_ple_gate"], w["w_ple"], layer, tm_res)
    return x2.reshape(b, t, d), kv, rest, cs_a, cs_g


def kernel(x_prompt, x_sample, cache_k, cache_v, state_pool, state_conv, page_table, p_prompt, p_sample,
           attn_norm_w, w_in, sb_bias, gmlp_ws, gmlp_b, pool_w, pool_scale, w_out, ffn_norm_w, w_up,
           conv_w, conv_b, w_down, ple_norm_w, w_ple, w_ple_gate, final_norm_w):
    depth = w_in.shape[0]
    bp, tp, d = x_prompt.shape
    bs, ts, _ = x_sample.shape
    n_pages = page_table.shape[1]
    page = cache_k.shape[2]
    past = n_pages * page
    d_ff = w_down.shape[1]
    ff_pad = -(-d_ff // FF_TN) * FF_TN

    cache_k = cache_k.reshape(depth, cache_k.shape[1], page * SB_HEADS, HEAD_DIM)
    cache_v = cache_v.reshape(depth, cache_v.shape[1], page * SB_HEADS, HEAD_DIM)

    def halves(a):
        a = a.reshape(a.shape[:-1] + (2, d_ff))
        return jnp.pad(a, [(0, 0)] * (a.ndim - 1) + [(0, ff_pad - d_ff)])

    w = {
        "attn_norm_w": attn_norm_w[:, None], "ffn_norm_w": ffn_norm_w[:, None], "ple_norm_w": ple_norm_w[:, None],
        "w_in": _to_bf16(w_in, CAST_ROWS),
        "w_out": _to_bf16(w_out, CAST_ROWS),
        "w_ple_gate": _to_bf16(w_ple_gate, CAST_ROWS),
        "w_ple": _to_bf16(w_ple, CAST_ROWS),
        "w_up": _to_bf16(w_up, CAST_ROWS, n_split=2, out_cols=ff_pad),
        "w_down": _to_bf16(w_down, FF_TN, out_rows=ff_pad),
        "conv_w": jnp.swapaxes(halves(conv_w), 1, 2),
        "conv_b": halves(conv_b)[:, :, None],
        "gmlp_ws": gmlp_ws, "gmlp_bt": jnp.swapaxes(gmlp_b, 1, 2), "pool_w": pool_w, "pool_scale": pool_scale[:, None],
    }
    p_prompt = p_prompt.reshape(depth, bp * tp, -1)
    p_sample = p_sample.reshape(depth, bs * ts, -1)
    pool_prev_s = jnp.pad(state_pool, ((0, 0), (0, 0), (POOL_HALO - POOL_STATE, 0), (0, 0)))
    conv_prev_s = halves(jnp.pad(state_conv, ((0, 0), (0, 0), (CONV_HALO - (CONV_WIDTH - 1), 0), (0, 0))))
    pool_prev_p = jnp.zeros((bp, POOL_HALO, POOL_WIDTH), F32)
    conv_prev_p = jnp.zeros((bp, CONV_HALO, ff_pad), F32)

    xp_, xs_ = x_prompt, x_sample
    outs = {k: [] for k in ("ks", "vs", "poolp", "pools", "convp", "convs", "gvs")}
    kv_prompt = [jnp.zeros((depth, bp * tp * SB_HEADS, HEAD_DIM), F32) for _ in range(2)]
    for l in range(depth):
        attn_p = functools.partial(_attn_prompt, sb_bias=sb_bias, layer=l, blk=_tile(tp, 256))
        xp_, kv_prompt, rest_p, csa_p, csg_p = _layer(
            xp_, p_prompt, w, l, pool_prev_p, conv_prev_p, conv_prev_p, 0, attn_p, kv_stack=kv_prompt)
        nbias_col = -jnp.repeat(sb_bias[l], ts)[:, None]

        def attn_s(qkv4):
            return _attn_sample(qkv4[0], qkv4[1], qkv4[2], nbias_col, cache_k, cache_v, page_table, l,
                                n_pp=_tile(n_pages, 8))

        xs_, (kv_s,), rest_s, csa_s, csg_s = _layer(
            xs_, p_sample, w, l, pool_prev_s[l], conv_prev_s[l, :, :, 0], conv_prev_s[l, :, :, 1], past, attn_s)

        outs["ks"].append(kv_s[0].reshape(bs, ts, SB_HEADS, HEAD_DIM))
        outs["vs"].append(kv_s[1].reshape(bs, ts, SB_HEADS, HEAD_DIM))
        xpool_p = rest_p.reshape(bp, tp, REST_WIDTH)[:, :, 2 * GMLP_WIDTH:]
        xpool_s = rest_s.reshape(bs, ts, REST_WIDTH)[:, :, 2 * GMLP_WIDTH:]
        outs["poolp"].append(xpool_p[:, -POOL_STATE:])
        outs["pools"].append(jnp.concatenate([state_pool[l], xpool_s], axis=1)[:, -POOL_STATE:])
        tail = slice(CONV_HALO - (CONV_WIDTH - 1), CONV_HALO)
        outs["convp"].append(jnp.concatenate([csa_p[:, -1, tail, :d_ff], csg_p[:, -1, tail, :d_ff]], axis=-1))
        outs["convs"].append(jnp.concatenate([csa_s[:, -1, tail, :d_ff], csg_s[:, -1, tail, :d_ff]], axis=-1))
        outs["gvs"].append(rest_s.reshape(bs, ts, REST_WIDTH)[:, :, GMLP_WIDTH:2 * GMLP_WIDTH])

    y_prompt = _final_norm(xp_.reshape(bp * tp, d), final_norm_w[None], _tile(bp * tp, 512)).reshape(bp, tp, d)
    y_sample = _final_norm(xs_.reshape(bs * ts, d), final_norm_w[None], bs * ts).reshape(bs, ts, d)
    k_prompt, v_prompt = (a.reshape(depth, bp, tp, SB_HEADS, HEAD_DIM) for a in kv_prompt)
    return (y_prompt, y_sample,
            k_prompt, v_prompt, jnp.stack(outs["ks"]), jnp.stack(outs["vs"]),
            jnp.stack(outs["poolp"]), jnp.stack(outs["pools"]),
            jnp.stack(outs["convp"]), jnp.stack(outs["convs"]), jnp.stack(outs["gvs"]))
```

```python
import functools

import jax
import jax.numpy as jnp
from jax import lax
from jax.experimental import pallas as pl
from jax.experimental.pallas import tpu as pltpu

F32 = jnp.float32
BF16 = jnp.bfloat16

EPS = 1e-6
HEAD_DIM = 128
SB_HEADS = 8
SB_WIDTH = SB_HEADS * HEAD_DIM
GMLP_HEADS = 4
CHUNK = 128
GMLP_WIDTH = GMLP_HEADS * CHUNK
POOL_WINDOWS = (2, 4, 8, 16)
POOL_CH = 128
POOL_WIDTH = POOL_CH * len(POOL_WINDOWS)
POOL_STATE = max(POOL_WINDOWS) - 1
POOL_HALO = 16
CONV_WIDTH = 3
CONV_HALO = 8
REST_WIDTH = 2 * GMLP_WIDTH + POOL_WIDTH
SCALE = HEAD_DIM ** -0.5
LOG2E = 1.4426950408889634

IN_TN = 512
FF_TN = 512
CAST_ROWS = 256
V7X_VMEM_LIMIT = 56 * 1024 * 1024


def _params(sem):
    return pltpu.CompilerParams(dimension_semantics=sem, vmem_limit_bytes=V7X_VMEM_LIMIT)


def _rms_bf16(x, w):
    ms = jnp.mean(x * x, axis=-1, keepdims=True)
    return ((x * lax.rsqrt(ms + EPS)) * w).astype(BF16)


def _cast_kernel(x_ref, o_ref, *, rows_valid):
    x = x_ref[...]
    tr, c = x.shape
    if rows_valid is not None:
        r = pl.program_id(1) * tr + lax.broadcasted_iota(jnp.int32, (tr, 1), 0)
        x = jnp.where(r < rows_valid, x, 0.0)
    o_ref[:, 0:c] = x.astype(BF16)
    if o_ref.shape[1] > c:
        o_ref[:, c:] = jnp.zeros((tr, o_ref.shape[1] - c), BF16)


def _to_bf16(w, tr, n_split=1, out_rows=None, out_cols=None):
    n_l, r, c_all = w.shape
    c = c_all // n_split
    out_rows = out_rows or r
    out_cols = out_cols or c
    assert c_all == n_split * c and out_rows % tr == 0 and out_rows - r < tr and out_cols >= c
    out = pl.pallas_call(
        functools.partial(_cast_kernel, rows_valid=r if out_rows > r else None),
        grid=(n_l, out_rows // tr, n_split),
        in_specs=[pl.BlockSpec((None, tr, c), lambda l, i, s: (l, i, s))],
        out_specs=pl.BlockSpec((None, None, tr, out_cols), lambda l, i, s: (l, s, i, 0)),
        out_shape=jax.ShapeDtypeStruct((n_l, n_split, out_rows, out_cols), BF16),
        compiler_params=_params(("arbitrary", "arbitrary", "arbitrary")),
        name="to_bf16",
    )(w)
    return out if n_split > 1 else out.reshape(n_l, out_rows, out_cols)


def _inproj_kernel(x_ref, nw_ref, w_ref, *refs, stacked_kv):
    if stacked_kv:
        k_ref, v_ref, qkvb_ref, rest_ref, xn_ref = refs[2:]
    else:
        kv_ref, qkvb_ref, rest_ref, xn_ref = refs
    j = pl.program_id(1)
    tm = x_ref.shape[0]
    heads_per_tile = IN_TN // HEAD_DIM

    @pl.when(j == 0)
    def _():
        xn_ref[...] = _rms_bf16(x_ref[...], nw_ref[...])

    def tile():
        return jnp.dot(xn_ref[...], w_ref[...], preferred_element_type=F32)

    @pl.when(j < 2)
    def _():
        qkvb_ref[0] = tile().astype(BF16)

    if stacked_kv:
        for first_tile, ref in ((2, k_ref), (4, v_ref)):
            @pl.when((j == first_tile) | (j == first_tile + 1))
            def _(first_tile=first_tile, ref=ref):
                acc = tile()
                qkvb_ref[0] = acc.astype(BF16)
                first_head = (j - first_tile) * heads_per_tile
                for hh in range(heads_per_tile):
                    ref[pl.ds(first_head + hh, tm, stride=SB_HEADS), :] = acc[:, hh * HEAD_DIM:(hh + 1) * HEAD_DIM]
    else:
        @pl.when((j >= 2) & (j < 6))
        def _():
            acc = tile()
            qkvb_ref[0] = acc.astype(BF16)
            kv_ref[0] = acc

    @pl.when((j == 6) | (j == 7))
    def _():
        rest_ref[...] = jax.nn.gelu(tile())

    @pl.when(j == 8)
    def _():
        rest_ref[...] = tile()


def _inproj(x2, nw, w_in, layer, tm, kv_stack=None):
    r, d = x2.shape
    n_j = w_in.shape[2] // IN_TN
    assert 3 * SB_WIDTH == 6 * IN_TN and n_j == 9 and r % tm == 0

    def kv_map(i, j):
        c = jnp.clip(j - 2, 0, 3)
        return (c // 2, i, c % 2)

    def qkv_map(i, j):
        c = jnp.minimum(j, 5)
        return (c // 2, i, c % 2)

    in_specs = [
        pl.BlockSpec((tm, d), lambda i, j: (i, 0)),
        pl.BlockSpec((None, 1, d), lambda i, j: (layer, 0, 0)),
        pl.BlockSpec((None, d, IN_TN), lambda i, j: (layer, 0, j)),
    ]
    common_specs = [
        pl.BlockSpec((1, tm, IN_TN), qkv_map),
        pl.BlockSpec((tm, IN_TN), lambda i, j: (i, jnp.maximum(j - 6, 0))),
    ]
    common_shapes = [
        jax.ShapeDtypeStruct((3, r, SB_WIDTH), BF16),
        jax.ShapeDtypeStruct((r, REST_WIDTH), F32),
    ]
    if kv_stack is None:
        operands, aliases = (x2, nw, w_in), {}
        kv_specs = [pl.BlockSpec((1, tm, IN_TN), kv_map)]
        kv_shapes = [jax.ShapeDtypeStruct((2, r, SB_WIDTH), F32)]
    else:
        operands, aliases = (x2, nw, w_in) + tuple(kv_stack), {3: 0, 4: 1}
        in_specs += [pl.BlockSpec(memory_space=pl.ANY)] * 2
        kv_specs = [pl.BlockSpec((None, tm * SB_HEADS, HEAD_DIM), lambda i, j: (layer, i, 0))] * 2
        kv_shapes = [jax.ShapeDtypeStruct(a.shape, a.dtype) for a in kv_stack]
    return pl.pallas_call(
        functools.partial(_inproj_kernel, stacked_kv=kv_stack is not None),
        grid=(r // tm, n_j),
        in_specs=in_specs,
        out_specs=kv_specs + common_specs,
        out_shape=kv_shapes + common_shapes,
        input_output_aliases=aliases,
        scratch_shapes=[pltpu.VMEM((tm, d), BF16)],
        compiler_params=_params(("arbitrary", "arbitrary")),
        name="inproj",
    )(*operands)


def _sb_terms(s, nbias2):
    nz2 = s * (-SCALE * LOG2E) + nbias2
    neg_abs = lax.bitcast_convert_type(lax.bitcast_convert_type(nz2, jnp.uint32) | jnp.uint32(0x80000000), F32)
    l2 = jnp.log2(1.0 + jnp.exp2(neg_abs))
    lk2 = jnp.minimum(nz2, 0.0) - l2
    lb2 = lk2 - nz2
    return lb2, lk2


def _suffix_sum(lk, u):
    return jnp.dot(lk.astype(BF16), u, preferred_element_type=F32)


def _later_key_matrix(n):
    row = lax.broadcasted_iota(jnp.int32, (n, n), 0)
    col = lax.broadcasted_iota(jnp.int32, (n, n), 1)
    return row, col, jnp.where(row > col, 1.0, 0.0).astype(BF16)


def _nt_dot(a, b):
    return lax.dot_general(a, b, (((1,), (1,)), ((), ())), preferred_element_type=F32)


def _attn_prompt_kernel(bias_ref, q_ref, k_ref, v_ref, o_ref, *, blk, layer):
    qi = pl.program_id(1)
    row, col, u = _later_key_matrix(blk)
    vis = col < row
    heads = [slice(h * HEAD_DIM, (h + 1) * HEAD_DIM) for h in range(SB_HEADS)]
    nbias2 = [bias_ref[layer, h] * (-LOG2E) for h in range(SB_HEADS)]

    def block(kb, state, masked):
        keys = pl.ds(pl.multiple_of(kb * blk, blk), blk)
        new_state = [None] * SB_HEADS

        def scores(h):
            return _nt_dot(q_ref[0, 0, :, heads[h]], k_ref[0, 0, keys, heads[h]])

        def finish(h, lb, lk, after):
            carry, acc = state[h]
            w = jnp.exp2(lb + after + carry)
            if masked:
                w = jnp.where(vis, w, 0.0)
            acc = acc + jnp.dot(w.astype(BF16), v_ref[0, 0, keys, heads[h]], preferred_element_type=F32)
            new_state[h] = (carry + after[:, 0:1] + lk[:, 0:1], acc)

        s_next = scores(0)
        pending = None
        for h in range(SB_HEADS):
            s, s_next = s_next, (scores(h + 1) if h + 1 < SB_HEADS else None)
            lb, lk = _sb_terms(s, nbias2[h])
            if masked:
                lk = jnp.where(vis, lk, 0.0)
            after = _suffix_sum(lk, u)
            if pending is not None:
                finish(*pending)
            pending = (h, lb, lk, after)
        finish(*pending)
        return tuple(new_state)

    state = tuple((jnp.zeros((blk, 1), F32), jnp.zeros((blk, HEAD_DIM), F32)) for _ in heads)
    state = block(qi, state, True)
    state = lax.fori_loop(0, qi, lambda it, st: block(qi - 1 - it, st, False), state)
    for h, sl in enumerate(heads):
        o_ref[0, :, sl] = state[h][1].astype(o_ref.dtype)


def _attn_prompt(qkv4, sb_bias, layer, blk):
    _, b, t, _ = qkv4.shape
    assert t % blk == 0
    return pl.pallas_call(
        functools.partial(_attn_prompt_kernel, blk=blk, layer=layer),
        grid=(b, t // blk),
        in_specs=[
            pl.BlockSpec(memory_space=pltpu.SMEM),
            pl.BlockSpec((1, 1, blk, SB_WIDTH), lambda bi, qi: (0, bi, qi, 0)),
            pl.BlockSpec((1, 1, t, SB_WIDTH), lambda bi, qi: (1, bi, 0, 0)),
            pl.BlockSpec((1, 1, t, SB_WIDTH), lambda bi, qi: (2, bi, 0, 0)),
        ],
        out_specs=pl.BlockSpec((1, blk, SB_WIDTH), lambda bi, qi: (bi, qi, 0)),
        out_shape=jax.ShapeDtypeStruct((b, t, SB_WIDTH), BF16),
        compiler_params=_params(("arbitrary", "arbitrary")),
        name="attn_prompt",
    )(sb_bias, qkv4, qkv4, qkv4)


def _attn_sample_kernel(pt_ref, q_ref, kn_ref, vn_ref, nb_ref, *refs, n_pp, t_new):
    k_refs = refs[:n_pp]
    v_refs = refs[n_pp:2 * n_pp]
    o_ref = refs[2 * n_pp]
    qbd_ref, acc_ref, carry_ref = refs[2 * n_pp + 1:]
    step = pl.program_id(1)
    n_rows = SB_HEADS * t_new
    page = k_refs[0].shape[0] // SB_HEADS
    _, _, u = _later_key_matrix(page)
    nbias = nb_ref[...]

    def blocks(ks, vs, vis):
        qbd = qbd_ref[...]
        terms = [_sb_terms(_nt_dot(qbd, k), nbias) for k in ks]
        if vis is not None:
            terms = [(lb, jnp.where(vis, lk, 0.0)) for lb, lk in terms]
        after = [_suffix_sum(lk, u) for _, lk in terms]
        carry = carry_ref[...]
        acc = acc_ref[...]
        for (lb, lk), af, v in zip(terms, after, vs):
            w = jnp.exp2(lb + af + carry)
            if vis is not None:
                w = jnp.where(vis, w, 0.0)
            acc = acc + jnp.dot(w.astype(BF16), v, preferred_element_type=F32)
            carry = carry + af[:, 0:1] + lk[:, 0:1]
        carry_ref[...] = carry
        acc_ref[...] = acc

    @pl.when(step == 0)
    def _():
        q = q_ref[0].astype(F32)
        qt = jnp.concatenate([q] * SB_HEADS, axis=0)
        rh = lax.broadcasted_iota(jnp.int32, (n_rows, SB_WIDTH), 0) // t_new
        ch = lax.broadcasted_iota(jnp.int32, (n_rows, SB_WIDTH), 1) // HEAD_DIM
        qbd_ref[...] = jnp.where(rh == ch, qt, 0.0).astype(BF16)
        acc_ref[...] = jnp.zeros_like(acc_ref)
        carry_ref[...] = jnp.zeros_like(carry_ref)
        pad = jnp.zeros((page - t_new, SB_WIDTH), F32)
        kn = jnp.concatenate([kn_ref[0].astype(F32), pad], axis=0).astype(BF16)
        vn = jnp.concatenate([vn_ref[0].astype(F32), pad], axis=0).astype(BF16)
        rt = lax.broadcasted_iota(jnp.int32, (n_rows, page), 0) % t_new
        ck = lax.broadcasted_iota(jnp.int32, (n_rows, page), 1)
        blocks([kn], [vn], ck < rt)

    def heads_on_lanes(ref):
        return jnp.concatenate(
            [ref[pl.ds(h, page, stride=SB_HEADS), :].astype(BF16) for h in range(SB_HEADS)], axis=1)

    blocks([heads_on_lanes(r) for r in k_refs], [heads_on_lanes(r) for r in v_refs], None)

    @pl.when(step == pl.num_programs(1) - 1)
    def _():
        for h in range(SB_HEADS):
            o_ref[0, :, h * HEAD_DIM:(h + 1) * HEAD_DIM] = acc_ref[
                h * t_new:(h + 1) * t_new, h * HEAD_DIM:(h + 1) * HEAD_DIM].astype(o_ref.dtype)


def _attn_sample(q3, kn3, vn3, nbias_col, cache_k, cache_v, page_table, layer, n_pp):
    b, t_new, _ = q3.shape
    n_pages = page_table.shape[1]
    page = cache_k.shape[2] // SB_HEADS
    assert n_pages % n_pp == 0 and t_new <= page
    n_rows = SB_HEADS * t_new

    def kv_spec(p):
        return pl.BlockSpec(
            (None, None, page * SB_HEADS, HEAD_DIM),
            lambda bi, s, pt: (layer, pt[bi, n_pages - 1 - (s * n_pp + p)], 0, 0))

    row_spec = pl.BlockSpec((1, t_new, SB_WIDTH), lambda bi, s, pt: (bi, 0, 0))
    grid_spec = pltpu.PrefetchScalarGridSpec(
        num_scalar_prefetch=1,
        grid=(b, n_pages // n_pp),
        in_specs=[row_spec, row_spec, row_spec,
                  pl.BlockSpec((n_rows, 1), lambda bi, s, pt: (0, 0))]
        + [kv_spec(p) for p in range(n_pp)] + [kv_spec(p) for p in range(n_pp)],
        out_specs=row_spec,
        scratch_shapes=[pltpu.VMEM((n_rows, SB_WIDTH), BF16),
                        pltpu.VMEM((n_rows, SB_WIDTH), F32),
                        pltpu.VMEM((n_rows, 1), F32)],
    )
    return pl.pallas_call(
        functools.partial(_attn_sample_kernel, n_pp=n_pp, t_new=t_new),
        grid_spec=grid_spec,
        out_shape=jax.ShapeDtypeStruct((b, t_new, SB_WIDTH), BF16),
        compiler_params=_params(("arbitrary", "arbitrary")),
        name="attn_sample",
    )(page_table, q3, kn3, vn3, nbias_col, *([cache_k] * n_pp), *([cache_v] * n_pp))


def _mix_kernel(rest_ref, prev_ref, ws_ref, gbt_ref, pw_ref, ps_ref, o_ref, pbuf_ref, *, nb, tr, pos0):
    ti = pl.program_id(1)

    @pl.when(ti == 0)
    def _():
        pbuf_ref[:, 0:POOL_HALO] = prev_ref[...]

    @pl.when(ti > 0)
    def _():
        pbuf_ref[:, 0:POOL_HALO] = pbuf_ref[:, tr:tr + POOL_HALO]

    pbuf_ref[:, POOL_HALO:] = rest_ref[:, :, 2 * GMLP_WIDTH:]

    tc = min(tr, CHUNK)
    row, col, _ = _later_key_matrix(CHUNK)
    causal = row >= col
    pos = (pos0 + ti * tr + lax.broadcasted_iota(jnp.int32, (tr, 1), 0) + 1).astype(F32)

    wsm_heads = [jnp.where(causal, ws_ref[g], 0.0).astype(BF16) for g in range(GMLP_HEADS)]

    for bb in range(nb):
        for g in range(GMLP_HEADS):
            sl = slice(g * CHUNK, (g + 1) * CHUNK)
            wsm = wsm_heads[g]
            bias = gbt_ref[0:tc, g:g + 1]
            for c in range(tr // tc):
                rs = slice(c * tc, (c + 1) * tc)
                u = rest_ref[bb, rs, sl]
                v = rest_ref[bb, rs, GMLP_WIDTH + g * CHUNK:GMLP_WIDTH + (g + 1) * CHUNK]
                if tc < CHUNK:
                    v = jnp.concatenate([v, jnp.zeros((CHUNK - tc, CHUNK), F32)], axis=0)
                mixed = jnp.dot(wsm, v.astype(BF16), preferred_element_type=F32)[0:tc] + bias
                o_ref[bb, rs, sl] = (u * mixed).astype(o_ref.dtype)
        for g, win in enumerate(POOL_WINDOWS):
            sl = slice(g * POOL_CH, (g + 1) * POOL_CH)
            x = pbuf_ref[bb, POOL_HALO:POOL_HALO + tr, sl]
            tot = x
            for d in range(1, win):
                tot = tot + pbuf_ref[bb, POOL_HALO - d:POOL_HALO - d + tr, sl]
            pooled = tot / jnp.minimum(pos, float(win)) - x
            mapped = jnp.dot(pooled.astype(BF16), pw_ref[g].astype(BF16), preferred_element_type=F32)
            o_ref[bb, :, GMLP_WIDTH + g * POOL_CH:GMLP_WIDTH + (g + 1) * POOL_CH] = (
                mapped * ps_ref[:, sl]).astype(o_ref.dtype)


def _mix(rest3, pool_prev, gmlp_ws, gmlp_bt, pool_w, pool_scale, layer, nb, tr, pos0):
    b, t, _ = rest3.shape
    assert b % nb == 0 and t % tr == 0 and (tr % CHUNK == 0 or (tr < CHUNK and t == tr))
    const = lambda *shape: pl.BlockSpec((None,) + shape, lambda bi, ti: (layer,) + (0,) * len(shape))
    return pl.pallas_call(
        functools.partial(_mix_kernel, nb=nb, tr=tr, pos0=pos0),
        grid=(b // nb, t // tr),
        in_specs=[
            pl.BlockSpec((nb, tr, REST_WIDTH), lambda bi, ti: (bi, ti, 0)),
            pl.BlockSpec((nb, POOL_HALO, POOL_WIDTH), lambda bi, ti: (bi, 0, 0)),
            const(GMLP_HEADS, CHUNK, CHUNK),
            const(CHUNK, GMLP_HEADS),
            const(len(POOL_WINDOWS), POOL_CH, POOL_CH),
            const(1, POOL_WIDTH),
        ],
        out_specs=pl.BlockSpec((nb, tr, GMLP_WIDTH + POOL_WIDTH), lambda bi, ti: (bi, ti, 0)),
        out_shape=jax.ShapeDtypeStruct((b, t, GMLP_WIDTH + POOL_WIDTH), BF16),
        scratch_shapes=[pltpu.VMEM((nb, POOL_HALO + tr, POOL_WIDTH), F32)],
        compiler_params=_params(("arbitrary", "arbitrary")),
        name="mix",
    )(rest3, pool_prev, gmlp_ws, gmlp_bt, pool_w, pool_scale)


def _outproj_kernel(x_ref, sb_ref, mix_ref, w_ref, o_ref):
    o_ref[...] = (x_ref[...]
                  + jnp.dot(sb_ref[...], w_ref[0:SB_WIDTH, :], preferred_element_type=F32)
                  + jnp.dot(mix_ref[...], w_ref[SB_WIDTH:, :], preferred_element_type=F32))


def _outproj(x2, o_sb, o_mix, w_out, layer, tm):
    r, d = x2.shape
    return pl.pallas_call(
        _outproj_kernel,
        grid=(r // tm,),
        in_specs=[
            pl.BlockSpec((tm, d), lambda i: (i, 0)),
            pl.BlockSpec((tm, SB_WIDTH), lambda i: (i, 0)),
            pl.BlockSpec((tm, GMLP_WIDTH + POOL_WIDTH), lambda i: (i, 0)),
            pl.BlockSpec((None,) + w_out.shape[1:], lambda i: (layer, 0, 0)),
        ],
        out_specs=pl.BlockSpec((tm, d), lambda i: (i, 0)),
        out_shape=jax.ShapeDtypeStruct((r, d), F32),
        compiler_params=_params(("arbitrary",)),
        name="outproj",
    )(x2, o_sb, o_mix, w_out)


def _ffn_kernel(x_ref, nw_ref, wa_ref, wg_ref, cwa_ref, cwg_ref, cba_ref, cbg_ref, wd_ref,
                pa_ref, pg_ref, y_ref, csa_ref, csg_ref,
                xn_ref, acc_ref, bufa_ref, bufg_ref, haloa_ref, halog_ref, *, nb, tr):
    bi = pl.program_id(0)
    ti = pl.program_id(1)
    j = pl.program_id(2)
    n_j = pl.num_programs(2) - 1
    d = x_ref.shape[-1]
    tn = wa_ref.shape[-1]
    halves = ((cwa_ref, cba_ref, pa_ref, csa_ref, bufa_ref, haloa_ref),
              (cwg_ref, cbg_ref, pg_ref, csg_ref, bufg_ref, halog_ref))

    def up_tile():
        xn = xn_ref[...]
        return tuple(jnp.dot(xn, w_ref[...], preferred_element_type=F32).reshape(nb, tr, tn)
                     for w_ref in (wa_ref, wg_ref))

    def park(ups):
        bufa_ref[:, CONV_HALO:] = ups[0]
        bufg_ref[:, CONV_HALO:] = ups[1]

    def down_tile():
        jj = j - 1
        outs = []
        for cw_ref, cb_ref, prev_ref, cs_ref, buf_ref, halo_ref in halves:
            buf_ref[:, 0:CONV_HALO] = jnp.where(ti == 0, prev_ref[...], halo_ref[jj])
            last = buf_ref[:, tr:tr + CONV_HALO]
            halo_ref[jj] = last
            cs_ref[:, 0] = last
            out = cb_ref[...]
            for i in range(CONV_WIDTH):
                lo = CONV_HALO - (CONV_WIDTH - 1) + i
                out = out + cw_ref[i:i + 1, :] * buf_ref[:, lo:lo + tr]
            outs.append(out.reshape(nb * tr, tn))
        a, g = outs
        h = (jax.nn.silu(g) * a).astype(BF16)
        return jnp.dot(h, wd_ref[...], preferred_element_type=F32)

    @pl.when((bi == 0) & (ti == 0) & (j == 0))
    def _():
        haloa_ref[...] = jnp.zeros_like(haloa_ref)
        halog_ref[...] = jnp.zeros_like(halog_ref)

    @pl.when(j == 0)
    def _():
        xn_ref[...] = _rms_bf16(x_ref[...].reshape(nb * tr, d), nw_ref[...])
        park(up_tile())

    @pl.when(j == 1)
    def _():
        ups = up_tile()
        acc_ref[...] = down_tile()
        park(ups)

    @pl.when((j > 1) & (j < n_j))
    def _():
        ups = up_tile()
        acc_ref[...] += down_tile()
        park(ups)

    @pl.when(j == n_j)
    def _():
        y_ref[...] = x_ref[...] + (acc_ref[...] + down_tile()).reshape(nb, tr, d)


def _ffn(x3, nw, w_up, conv_w, conv_b, w_down, prev_a, prev_g, layer, nb, tr):
    b, t, d = x3.shape
    npad = w_up.shape[3]
    n_j = npad // FF_TN
    n_t = t // tr
    assert b % nb == 0 and t % tr == 0 and tr >= CONV_HALO and npad % FF_TN == 0 and n_j >= 2
    up_j = lambda j: jnp.minimum(j, n_j - 1)
    dn_j = lambda j: jnp.maximum(j - 1, 0)
    up = lambda half: pl.BlockSpec((None, None, d, FF_TN), lambda bi, ti, j: (layer, half, 0, up_j(j)))
    col = lambda rows, half: pl.BlockSpec((None, None, rows, FF_TN), lambda bi, ti, j: (layer, half, 0, dn_j(j)))
    halo = pl.BlockSpec((nb, CONV_HALO, FF_TN), lambda bi, ti, j: (bi, 0, dn_j(j)))
    last = pl.BlockSpec((nb, 1, CONV_HALO, FF_TN), lambda bi, ti, j: (bi, ti, 0, dn_j(j)))
    xs = pl.BlockSpec((nb, tr, d), lambda bi, ti, j: (bi, ti, 0))
    return pl.pallas_call(
        functools.partial(_ffn_kernel, nb=nb, tr=tr),
        grid=(b // nb, n_t, n_j + 1),
        in_specs=[
            xs,
            pl.BlockSpec((None, 1, d), lambda bi, ti, j: (layer, 0, 0)),
            up(0), up(1),
            col(CONV_WIDTH, 0), col(CONV_WIDTH, 1), col(1, 0), col(1, 1),
            pl.BlockSpec((None, FF_TN, d), lambda bi, ti, j: (layer, dn_j(j), 0)),
            halo, halo,
        ],
        out_specs=[xs, last, last],
        out_shape=[
            jax.ShapeDtypeStruct((b, t, d), F32),
            jax.ShapeDtypeStruct((b, n_t, CONV_HALO, npad), F32),
            jax.ShapeDtypeStruct((b, n_t, CONV_HALO, npad), F32),
        ],
        scratch_shapes=[
            pltpu.VMEM((nb * tr, d), BF16),
            pltpu.VMEM((nb * tr, d), F32),
            pltpu.VMEM((nb, CONV_HALO + tr, FF_TN), F32),
            pltpu.VMEM((nb, CONV_HALO + tr, FF_TN), F32),
            pltpu.VMEM((n_j, nb, CONV_HALO, FF_TN), F32),
            pltpu.VMEM((n_j, nb, CONV_HALO, FF_TN), F32),
        ],
        compiler_params=_params(("arbitrary", "arbitrary", "arbitrary")),
        name="ffn",
    )(x3, nw, w_up, w_up, conv_w, conv_w, conv_b, conv_b, w_down, prev_a, prev_g)


def _ple_kernel(x_ref, p_ref, nw_ref, wg_ref, wp_ref, o_ref):
    x = x_ref[...]
    gate = jax.nn.sigmoid(jnp.dot(_rms_bf16(x, nw_ref[...]), wg_ref[...], preferred_element_type=F32))
    emb = jnp.dot(p_ref[...].astype(BF16), wp_ref[...], preferred_element_type=F32)
    o_ref[...] = x + emb * gate


def _ple(x2, p, nw, w_gate, w_ple, layer, tm):
    r, d = x2.shape
    whole = lambda a: pl.BlockSpec((None,) + a.shape[1:], lambda i: (layer, 0, 0))
    return pl.pallas_call(
        _ple_kernel,
        grid=(r // tm,),
        in_specs=[
            pl.BlockSpec((tm, d), lambda i: (i, 0)),
            pl.BlockSpec((None, tm, p.shape[2]), lambda i: (layer, i, 0)),
            whole(nw), whole(w_gate), whole(w_ple),
        ],
        out_specs=pl.BlockSpec((tm, d), lambda i: (i, 0)),
        out_shape=jax.ShapeDtypeStruct((r, d), F32),
        compiler_params=_params(("arbitrary",)),
        name="ple",
    )(x2, p, nw, w_gate, w_ple)


def _norm_kernel(x_ref, nw_ref, o_ref):
    x = x_ref[...]
    ms = jnp.mean(x * x, axis=-1, keepdims=True)
    o_ref[...] = (x * lax.rsqrt(ms + EPS)) * nw_ref[...]


def _final_norm(x2, nw, tm):
    r, d = x2.shape
    return pl.pallas_call(
        _norm_kernel,
        grid=(r // tm,),
        in_specs=[pl.BlockSpec((tm, d), lambda i: (i, 0)), pl.BlockSpec((1, d), lambda i: (0, 0))],
        out_specs=pl.BlockSpec((tm, d), lambda i: (i, 0)),
        out_shape=jax.ShapeDtypeStruct((r, d), F32),
        compiler_params=_params(("arbitrary",)),
        name="final_norm",
    )(x2, nw)


def _tile(n, cap):
    t = min(n, cap)
    while n % t:
        t //= 2
    return t


def _layer(x3, p, w, layer, pool_prev, conv_prev_a, conv_prev_g, pos0, attn_fn, kv_stack=None):
    b, t, d = x3.shape
    r = b * t
    one_tile = t < CHUNK
    nb, tr = (b, t) if one_tile else (1, _tile(t, 512))
    tm = r if one_tile else _tile(r, 1024)

    *kv, qkv, rest = _inproj(x3.reshape(r, d), w["attn_norm_w"], w["w_in"], layer, tm, kv_stack)
    o_sb = attn_fn(qkv.reshape(3, b, t, SB_WIDTH))
    o_mix = _mix(rest.reshape(b, t, REST_WIDTH), pool_prev, w["gmlp_ws"], w["gmlp_bt"],
                 w["pool_w"], w["pool_scale"], layer, nb, tr, pos0)
    tm_res = r if one_tile else _tile(r, 512)
    x2 = _outproj(x3.reshape(r, d), o_sb.reshape(r, SB_WIDTH), o_mix.reshape(r, -1), w["w_out"], layer, tm_res)
    x3, cs_a, cs_g = _ffn(x2.reshape(b, t, d), w["ffn_norm_w"], w["w_up"], w["conv_w"], w["conv_b"], w["w_down"],
                          conv_prev_a, conv_prev_g, layer, nb, tr)
    x2 = _ple(x3.reshape(r, d), p, w["ple_norm_w"], w["w_ple_gate"], w["w_ple"], layer, tm_res)
    return x2.reshape(b, t, d), kv, rest, cs_a, cs_g


def kernel(x_prompt, x_sample, cache_k, cache_v, state_pool, state_conv, page_table, p_prompt, p_sample,
           attn_norm_w, w_in, sb_bias, gmlp_ws, gmlp_b, pool_w, pool_scale, w_out, ffn_norm_w, w_up,
           conv_w, conv_b, w_down, ple_norm_w, w_ple, w_ple_gate, final_norm_w):
    depth = w_in.shape[0]
    bp, tp, d = x_prompt.shape
    bs, ts, _ = x_sample.shape
    n_pages = page_table.shape[1]
    page = cache_k.shape[2]
    past = n_pages * page
    d_ff = w_down.shape[1]
    ff_pad = -(-d_ff // FF_TN) * FF_TN

    cache_k = cache_k.reshape(depth, cache_k.shape[1], page * SB_HEADS, HEAD_DIM)
    cache_v = cache_v.reshape(depth, cache_v.shape[1], page * SB_HEADS, HEAD_DIM)

    def halves(a):
        a = a.reshape(a.shape[:-1] + (2, d_ff))
        return jnp.pad(a, [(0, 0)] * (a.ndim - 1) + [(0, ff_pad - d_ff)])

    w = {
        "attn_norm_w": attn_norm_w[:, None], "ffn_norm_w": ffn_norm_w[:, None], "ple_norm_w": ple_norm_w[:, None],
        "w_in": _to_bf16(w_in, CAST_ROWS),
        "w_out": _to_bf16(w_out, CAST_ROWS),
        "w_ple_gate": _to_bf16(w_ple_gate, CAST_ROWS),
        "w_ple": _to_bf16(w_ple, CAST_ROWS),
        "w_up": _to_bf16(w_up, CAST_ROWS, n_split=2, out_cols=ff_pad),
        "w_down": _to_bf16(w_down, FF_TN, out_rows=ff_pad),
        "conv_w": jnp.swapaxes(halves(conv_w), 1, 2),
        "conv_b": halves(conv_b)[:, :, None],
        "gmlp_ws": gmlp_ws, "gmlp_bt": jnp.swapaxes(gmlp_b, 1, 2), "pool_w": pool_w, "pool_scale": pool_scale[:, None],
    }
    p_prompt = p_prompt.reshape(depth, bp * tp, -1)
    p_sample = p_sample.reshape(depth, bs * ts, -1)
    pool_prev_s = jnp.pad(state_pool, ((0, 0), (0, 0), (POOL_HALO - POOL_STATE, 0), (0, 0)))
    conv_prev_s = halves(jnp.pad(state_conv, ((0, 0), (0, 0), (CONV_HALO - (CONV_WIDTH - 1), 0), (0, 0))))
    pool_prev_p = jnp.zeros((bp, POOL_HALO, POOL_WIDTH), F32)
    conv_prev_p = jnp.zeros((bp, CONV_HALO, ff_pad), F32)

    xp_, xs_ = x_prompt, x_sample
    outs = {k: [] for k in ("ks", "vs", "poolp", "pools", "convp", "convs", "gvs")}
    kv_prompt = [jnp.zeros((depth, bp * tp * SB_HEADS, HEAD_DIM), F32) for _ in range(2)]
    for l in range(depth):
        attn_p = functools.partial(_attn_prompt, sb_bias=sb_bias, layer=l, blk=_tile(tp, 256))
        xp_, kv_prompt, rest_p, csa_p, csg_p = _layer(
            xp_, p_prompt, w, l, pool_prev_p, conv_prev_p, conv_prev_p, 0, attn_p, kv_stack=kv_prompt)
        nbias_col = jnp.repeat(sb_bias[l], ts)[:, None] * (-LOG2E)

        def attn_s(qkv4):
            return _attn_sample(qkv4[0], qkv4[1], qkv4[2], nbias_col, cache_k, cache_v, page_table, l,
                                n_pp=_tile(n_pages, 8))

        xs_, (kv_s,), rest_s, csa_s, csg_s = _layer(
            xs_, p_sample, w, l, pool_prev_s[l], conv_prev_s[l, :, :, 0], conv_prev_s[l, :, :, 1], past, attn_s)

        outs["ks"].append(kv_s[0].reshape(bs, ts, SB_HEADS, HEAD_DIM))
        outs["vs"].append(kv_s[1].reshape(bs, ts, SB_HEADS, HEAD_DIM))
        xpool_p = rest_p.reshape(bp, tp, REST_WIDTH)[:, :, 2 * GMLP_WIDTH:]
        xpool_s = rest_s.reshape(bs, ts, REST_WIDTH)[:, :, 2 * GMLP_WIDTH:]
        outs["poolp"].append(xpool_p[:, -POOL_STATE:])
        outs["pools"].append(jnp.concatenate([state_pool[l], xpool_s], axis=1)[:, -POOL_STATE:])
        tail = slice(CONV_HALO - (CONV_WIDTH - 1), CONV_HALO)
        outs["convp"].append(jnp.concatenate([csa_p[:, -1, tail, :d_ff], csg_p[:, -1, tail, :d_ff]], axis=-1))
        outs["convs"].append(jnp.concatenate([csa_s[:, -1, tail, :d_ff], csg_s[:, -1, tail, :d_ff]], axis=-1))
        outs["gvs"].append(rest_s.reshape(bs, ts, REST_WIDTH)[:, :, GMLP_WIDTH:2 * GMLP_WIDTH])

    y_prompt = _final_norm(xp_.reshape(bp * tp, d), final_norm_w[None], _tile(bp * tp, 512)).reshape(bp, tp, d)
    y_sample = _final_norm(xs_.reshape(bs * ts, d), final_norm_w[None], bs * ts).reshape(bs, ts, d)
    k_prompt, v_prompt = (a.reshape(depth, bp, tp, SB_HEADS, HEAD_DIM) for a in kv_prompt)
    return (y_prompt, y_sample,
            k_prompt, v_prompt, jnp.stack(outs["ks"]), jnp.stack(outs["vs"]),
            jnp.stack(outs["poolp"]), jnp.stack(outs["pools"]),
            jnp.stack(outs["convp"]), jnp.stack(outs["convs"]), jnp.stack(outs["gvs"]))
```

```python
import functools

import jax
import jax.numpy as jnp
from jax import lax
from jax.experimental import pallas as pl
from jax.experimental.pallas import tpu as pltpu

F32 = jnp.float32
BF16 = jnp.bfloat16

EPS = 1e-6
HEAD_DIM = 128
SB_HEADS = 8
SB_WIDTH = SB_HEADS * HEAD_DIM
GMLP_HEADS = 4
CHUNK = 128
GMLP_WIDTH = GMLP_HEADS * CHUNK
POOL_WINDOWS = (2, 4, 8, 16)
POOL_CH = 128
POOL_WIDTH = POOL_CH * len(POOL_WINDOWS)
POOL_STATE = max(POOL_WINDOWS) - 1
POOL_HALO = 16
CONV_WIDTH = 3
CONV_HALO = 8
REST_WIDTH = 2 * GMLP_WIDTH + POOL_WIDTH
SCALE = HEAD_DIM ** -0.5
LOG2E = 1.4426950408889634

IN_TN = 512
FF_TN = 512
CAST_ROWS = 256
SAMPLE_PAGES_PER_STEP = 16
ATTN_BLOCK = 256
INPROJ_ROWS = 1024
MIX_ROWS = 512
RESIDUAL_ROWS = 512
FFN_ROWS = 1024
F32_SIGN_BIT = 0x80000000
V7X_VMEM_LIMIT = 56 * 1024 * 1024


def _params(sem):
    return pltpu.CompilerParams(dimension_semantics=sem, vmem_limit_bytes=V7X_VMEM_LIMIT)


def _rms_bf16(x, w):
    ms = jnp.mean(x * x, axis=-1, keepdims=True)
    return ((x * lax.rsqrt(ms + EPS)) * w).astype(BF16)


def _cast_kernel(x_ref, o_ref, *, rows_valid):
    x = x_ref[...]
    tr, c = x.shape
    if rows_valid is not None:
        r = pl.program_id(1) * tr + lax.broadcasted_iota(jnp.int32, (tr, 1), 0)
        x = jnp.where(r < rows_valid, x, 0.0)
    o_ref[:, 0:c] = x.astype(BF16)
    if o_ref.shape[1] > c:
        o_ref[:, c:] = jnp.zeros((tr, o_ref.shape[1] - c), BF16)


def _to_bf16(w, tr, n_split=1, out_rows=None, out_cols=None):
    n_l, r, c_all = w.shape
    c = c_all // n_split
    out_rows = out_rows or r
    out_cols = out_cols or c
    assert c_all == n_split * c and out_rows % tr == 0 and out_rows - r < tr and out_cols >= c
    out = pl.pallas_call(
        functools.partial(_cast_kernel, rows_valid=r if out_rows > r else None),
        grid=(n_l, out_rows // tr, n_split),
        in_specs=[pl.BlockSpec((None, tr, c), lambda l, i, s: (l, i, s))],
        out_specs=pl.BlockSpec((None, None, tr, out_cols), lambda l, i, s: (l, s, i, 0)),
        out_shape=jax.ShapeDtypeStruct((n_l, n_split, out_rows, out_cols), BF16),
        compiler_params=_params(("arbitrary", "arbitrary", "arbitrary")),
        name="to_bf16",
    )(w)
    return out if n_split > 1 else out.reshape(n_l, out_rows, out_cols)


def _inproj_kernel(x_ref, nw_ref, w_ref, *refs, stacked_kv):
    if stacked_kv:
        k_ref, v_ref, qkvb_ref, rest_ref, xn_ref = refs[2:]
    else:
        kv_ref, qkvb_ref, rest_ref, xn_ref = refs
    j = pl.program_id(1)
    tm = x_ref.shape[0]
    heads_per_tile = IN_TN // HEAD_DIM

    @pl.when(j == 0)
    def _():
        xn_ref[...] = _rms_bf16(x_ref[...], nw_ref[...])

    def tile():
        return jnp.dot(xn_ref[...], w_ref[...], preferred_element_type=F32)

    @pl.when(j < 2)
    def _():
        qkvb_ref[0] = tile().astype(BF16)

    if stacked_kv:
        for first_tile, ref in ((2, k_ref), (4, v_ref)):
            @pl.when((j == first_tile) | (j == first_tile + 1))
            def _(first_tile=first_tile, ref=ref):
                acc = tile()
                qkvb_ref[0] = acc.astype(BF16)
                first_head = (j - first_tile) * heads_per_tile
                for hh in range(heads_per_tile):
                    ref[pl.ds(first_head + hh, tm, stride=SB_HEADS), :] = acc[:, hh * HEAD_DIM:(hh + 1) * HEAD_DIM]
    else:
        @pl.when((j >= 2) & (j < 6))
        def _():
            acc = tile()
            qkvb_ref[0] = acc.astype(BF16)
            kv_ref[0] = acc

    @pl.when((j == 6) | (j == 7))
    def _():
        rest_ref[...] = jax.nn.gelu(tile())

    @pl.when(j == 8)
    def _():
        rest_ref[...] = tile()


def _inproj(x2, nw, w_in, layer, tm, kv_stack=None):
    r, d = x2.shape
    n_j = w_in.shape[2] // IN_TN
    assert 3 * SB_WIDTH == 6 * IN_TN and n_j == 9 and r % tm == 0

    def kv_map(i, j):
        c = jnp.clip(j - 2, 0, 3)
        return (c // 2, i, c % 2)

    def qkv_map(i, j):
        c = jnp.minimum(j, 5)
        return (c // 2, i, c % 2)

    in_specs = [
        pl.BlockSpec((tm, d), lambda i, j: (i, 0)),
        pl.BlockSpec((None, 1, d), lambda i, j: (layer, 0, 0)),
        pl.BlockSpec((None, d, IN_TN), lambda i, j: (layer, 0, j)),
    ]
    common_specs = [
        pl.BlockSpec((1, tm, IN_TN), qkv_map),
        pl.BlockSpec((tm, IN_TN), lambda i, j: (i, jnp.maximum(j - 6, 0))),
    ]
    common_shapes = [
        jax.ShapeDtypeStruct((3, r, SB_WIDTH), BF16),
        jax.ShapeDtypeStruct((r, REST_WIDTH), F32),
    ]
    if kv_stack is None:
        operands, aliases = (x2, nw, w_in), {}
        kv_specs = [pl.BlockSpec((1, tm, IN_TN), kv_map)]
        kv_shapes = [jax.ShapeDtypeStruct((2, r, SB_WIDTH), F32)]
    else:
        operands, aliases = (x2, nw, w_in) + tuple(kv_stack), {3: 0, 4: 1}
        in_specs += [pl.BlockSpec(memory_space=pl.ANY)] * 2
        kv_specs = [pl.BlockSpec((None, tm * SB_HEADS, HEAD_DIM), lambda i, j: (layer, i, 0))] * 2
        kv_shapes = [jax.ShapeDtypeStruct(a.shape, a.dtype) for a in kv_stack]
    return pl.pallas_call(
        functools.partial(_inproj_kernel, stacked_kv=kv_stack is not None),
        grid=(r // tm, n_j),
        in_specs=in_specs,
        out_specs=kv_specs + common_specs,
        out_shape=kv_shapes + common_shapes,
        input_output_aliases=aliases,
        scratch_shapes=[pltpu.VMEM((tm, d), BF16)],
        compiler_params=_params(("arbitrary", "arbitrary")),
        name="inproj",
    )(*operands)


def _sb_terms(s, nbias2):
    nz2 = s * (-SCALE * LOG2E) + nbias2
    neg_abs = lax.bitcast_convert_type(lax.bitcast_convert_type(nz2, jnp.uint32) | jnp.uint32(F32_SIGN_BIT), F32)
    l2 = jnp.log2(1.0 + jnp.exp2(neg_abs))
    lk2 = jnp.minimum(nz2, 0.0) - l2
    lb2 = lk2 - nz2
    return lb2, lk2


def _suffix_sum(lk, u):
    return jnp.dot(lk.astype(BF16), u, preferred_element_type=F32)


def _later_key_matrix(n):
    row = lax.broadcasted_iota(jnp.int32, (n, n), 0)
    col = lax.broadcasted_iota(jnp.int32, (n, n), 1)
    return row, col, jnp.where(row > col, 1.0, 0.0).astype(BF16)


def _nt_dot(a, b):
    return lax.dot_general(a, b, (((1,), (1,)), ((), ())), preferred_element_type=F32)


def _attn_prompt_kernel(bias_ref, q_ref, k_ref, v_ref, o_ref, *, blk, layer):
    qi = pl.program_id(1)
    row, col, u = _later_key_matrix(blk)
    vis = col < row
    heads = [slice(h * HEAD_DIM, (h + 1) * HEAD_DIM) for h in range(SB_HEADS)]
    nbias2 = [bias_ref[layer, h] * (-LOG2E) for h in range(SB_HEADS)]

    def block(kb, state, masked):
        keys = pl.ds(pl.multiple_of(kb * blk, blk), blk)
        new_state = [None] * SB_HEADS

        def scores(h):
            return _nt_dot(q_ref[0, 0, :, heads[h]], k_ref[0, 0, keys, heads[h]])

        def finish(h, lb, lk, after):
            carry, acc = state[h]
            w = jnp.exp2(lb + after + carry)
            if masked:
                w = jnp.where(vis, w, 0.0)
            acc = acc + jnp.dot(w.astype(BF16), v_ref[0, 0, keys, heads[h]], preferred_element_type=F32)
            new_state[h] = (carry + after[:, 0:1] + lk[:, 0:1], acc)

        s_next = scores(0)
        pending = None
        for h in range(SB_HEADS):
            s, s_next = s_next, (scores(h + 1) if h + 1 < SB_HEADS else None)
            lb, lk = _sb_terms(s, nbias2[h])
            if masked:
                lk = jnp.where(vis, lk, 0.0)
            after = _suffix_sum(lk, u)
            if pending is not None:
                finish(*pending)
            pending = (h, lb, lk, after)
        finish(*pending)
        return tuple(new_state)

    state = tuple((jnp.zeros((blk, 1), F32), jnp.zeros((blk, HEAD_DIM), F32)) for _ in heads)
    state = block(qi, state, True)
    state = lax.fori_loop(0, qi, lambda it, st: block(qi - 1 - it, st, False), state)
    for h, sl in enumerate(heads):
        o_ref[0, :, sl] = state[h][1].astype(o_ref.dtype)


def _attn_prompt(qkv4, sb_bias, layer, blk):
    _, b, t, _ = qkv4.shape
    assert t % blk == 0
    return pl.pallas_call(
        functools.partial(_attn_prompt_kernel, blk=blk, layer=layer),
        grid=(b, t // blk),
        in_specs=[
            pl.BlockSpec(memory_space=pltpu.SMEM),
            pl.BlockSpec((1, 1, blk, SB_WIDTH), lambda bi, qi: (0, bi, qi, 0)),
            pl.BlockSpec((1, 1, t, SB_WIDTH), lambda bi, qi: (1, bi, 0, 0)),
            pl.BlockSpec((1, 1, t, SB_WIDTH), lambda bi, qi: (2, bi, 0, 0)),
        ],
        out_specs=pl.BlockSpec((1, blk, SB_WIDTH), lambda bi, qi: (bi, qi, 0)),
        out_shape=jax.ShapeDtypeStruct((b, t, SB_WIDTH), BF16),
        compiler_params=_params(("arbitrary", "arbitrary")),
        name="attn_prompt",
    )(sb_bias, qkv4, qkv4, qkv4)


def _attn_sample_kernel(pt_ref, q_ref, kn_ref, vn_ref, nb_ref, *refs, n_pp, t_new):
    k_refs = refs[:n_pp]
    v_refs = refs[n_pp:2 * n_pp]
    o_ref = refs[2 * n_pp]
    qbd_ref, acc_ref, carry_ref = refs[2 * n_pp + 1:]
    step = pl.program_id(1)
    n_rows = SB_HEADS * t_new
    page = k_refs[0].shape[0] // SB_HEADS
    _, _, u = _later_key_matrix(page)
    nbias = nb_ref[...]

    def blocks(ks, vs, vis):
        qbd = qbd_ref[...]
        terms = [_sb_terms(_nt_dot(qbd, k), nbias) for k in ks]
        if vis is not None:
            terms = [(lb, jnp.where(vis, lk, 0.0)) for lb, lk in terms]
        after = [_suffix_sum(lk, u) for _, lk in terms]
        carry = carry_ref[...]
        acc = acc_ref[...]
        for (lb, lk), af, v in zip(terms, after, vs):
            w = jnp.exp2(lb + af + carry)
            if vis is not None:
                w = jnp.where(vis, w, 0.0)
            acc = acc + jnp.dot(w.astype(BF16), v, preferred_element_type=F32)
            carry = carry + af[:, 0:1] + lk[:, 0:1]
        carry_ref[...] = carry
        acc_ref[...] = acc

    @pl.when(step == 0)
    def _():
        q = q_ref[0].astype(F32)
        qt = jnp.concatenate([q] * SB_HEADS, axis=0)
        rh = lax.broadcasted_iota(jnp.int32, (n_rows, SB_WIDTH), 0) // t_new
        ch = lax.broadcasted_iota(jnp.int32, (n_rows, SB_WIDTH), 1) // HEAD_DIM
        qbd_ref[...] = jnp.where(rh == ch, qt, 0.0).astype(BF16)
        acc_ref[...] = jnp.zeros_like(acc_ref)
        carry_ref[...] = jnp.zeros_like(carry_ref)
        pad = jnp.zeros((page - t_new, SB_WIDTH), F32)
        kn = jnp.concatenate([kn_ref[0].astype(F32), pad], axis=0).astype(BF16)
        vn = jnp.concatenate([vn_ref[0].astype(F32), pad], axis=0).astype(BF16)
        rt = lax.broadcasted_iota(jnp.int32, (n_rows, page), 0) % t_new
        ck = lax.broadcasted_iota(jnp.int32, (n_rows, page), 1)
        blocks([kn], [vn], ck < rt)

    def heads_on_lanes(ref):
        return jnp.concatenate(
            [ref[pl.ds(h, page, stride=SB_HEADS), :].astype(BF16) for h in range(SB_HEADS)], axis=1)

    blocks([heads_on_lanes(r) for r in k_refs], [heads_on_lanes(r) for r in v_refs], None)

    @pl.when(step == pl.num_programs(1) - 1)
    def _():
        for h in range(SB_HEADS):
            o_ref[0, :, h * HEAD_DIM:(h + 1) * HEAD_DIM] = acc_ref[
                h * t_new:(h + 1) * t_new, h * HEAD_DIM:(h + 1) * HEAD_DIM].astype(o_ref.dtype)


def _attn_sample(q3, kn3, vn3, nbias_col, cache_k, cache_v, page_table, layer, n_pp):
    b, t_new, _ = q3.shape
    n_pages = page_table.shape[1]
    page = cache_k.shape[2] // SB_HEADS
    assert n_pages % n_pp == 0 and t_new <= page
    n_rows = SB_HEADS * t_new

    def kv_spec(p):
        return pl.BlockSpec(
            (None, None, page * SB_HEADS, HEAD_DIM),
            lambda bi, s, pt: (layer, pt[bi, n_pages - 1 - (s * n_pp + p)], 0, 0))

    row_spec = pl.BlockSpec((1, t_new, SB_WIDTH), lambda bi, s, pt: (bi, 0, 0))
    grid_spec = pltpu.PrefetchScalarGridSpec(
        num_scalar_prefetch=1,
        grid=(b, n_pages // n_pp),
        in_specs=[row_spec, row_spec, row_spec,
                  pl.BlockSpec((n_rows, 1), lambda bi, s, pt: (0, 0))]
        + [kv_spec(p) for p in range(n_pp)] + [kv_spec(p) for p in range(n_pp)],
        out_specs=row_spec,
        scratch_shapes=[pltpu.VMEM((n_rows, SB_WIDTH), BF16),
                        pltpu.VMEM((n_rows, SB_WIDTH), F32),
                        pltpu.VMEM((n_rows, 1), F32)],
    )
    return pl.pallas_call(
        functools.partial(_attn_sample_kernel, n_pp=n_pp, t_new=t_new),
        grid_spec=grid_spec,
        out_shape=jax.ShapeDtypeStruct((b, t_new, SB_WIDTH), BF16),
        compiler_params=_params(("arbitrary", "arbitrary")),
        name="attn_sample",
    )(page_table, q3, kn3, vn3, nbias_col, *([cache_k] * n_pp), *([cache_v] * n_pp))


def _mix_kernel(rest_ref, prev_ref, ws_ref, gbt_ref, pw_ref, ps_ref, o_ref, pbuf_ref, *, nb, tr, pos0):
    ti = pl.program_id(1)

    @pl.when(ti == 0)
    def _():
        pbuf_ref[:, 0:POOL_HALO] = prev_ref[...]

    @pl.when(ti > 0)
    def _():
        pbuf_ref[:, 0:POOL_HALO] = pbuf_ref[:, tr:tr + POOL_HALO]

    pbuf_ref[:, POOL_HALO:] = rest_ref[:, :, 2 * GMLP_WIDTH:]

    tc = min(tr, CHUNK)
    row, col, _ = _later_key_matrix(CHUNK)
    causal = row >= col
    pos = (pos0 + ti * tr + lax.broadcasted_iota(jnp.int32, (tr, 1), 0) + 1).astype(F32)

    wsm_heads = [jnp.where(causal, ws_ref[g], 0.0).astype(BF16) for g in range(GMLP_HEADS)]

    for bb in range(nb):
        for g in range(GMLP_HEADS):
            sl = slice(g * CHUNK, (g + 1) * CHUNK)
            wsm = wsm_heads[g]
            bias = gbt_ref[0:tc, g:g + 1]
            for c in range(tr // tc):
                rs = slice(c * tc, (c + 1) * tc)
                u = rest_ref[bb, rs, sl]
                v = rest_ref[bb, rs, GMLP_WIDTH + g * CHUNK:GMLP_WIDTH + (g + 1) * CHUNK]
                if tc < CHUNK:
                    v = jnp.concatenate([v, jnp.zeros((CHUNK - tc, CHUNK), F32)], axis=0)
                mixed = jnp.dot(wsm, v.astype(BF16), preferred_element_type=F32)[0:tc] + bias
                o_ref[bb, rs, sl] = (u * mixed).astype(o_ref.dtype)
        for g, win in enumerate(POOL_WINDOWS):
            sl = slice(g * POOL_CH, (g + 1) * POOL_CH)
            x = pbuf_ref[bb, POOL_HALO:POOL_HALO + tr, sl]
            tot = x
            for d in range(1, win):
                tot = tot + pbuf_ref[bb, POOL_HALO - d:POOL_HALO - d + tr, sl]
            pooled = tot / jnp.minimum(pos, float(win)) - x
            mapped = jnp.dot(pooled.astype(BF16), pw_ref[g].astype(BF16), preferred_element_type=F32)
            o_ref[bb, :, GMLP_WIDTH + g * POOL_CH:GMLP_WIDTH + (g + 1) * POOL_CH] = (
                mapped * ps_ref[:, sl]).astype(o_ref.dtype)


def _mix(rest3, pool_prev, gmlp_ws, gmlp_bt, pool_w, pool_scale, layer, nb, tr, pos0):
    b, t, _ = rest3.shape
    assert b % nb == 0 and t % tr == 0 and (tr % CHUNK == 0 or (tr < CHUNK and t == tr))
    const = lambda *shape: pl.BlockSpec((None,) + shape, lambda bi, ti: (layer,) + (0,) * len(shape))
    return pl.pallas_call(
        functools.partial(_mix_kernel, nb=nb, tr=tr, pos0=pos0),
        grid=(b // nb, t // tr),
        in_specs=[
            pl.BlockSpec((nb, tr, REST_WIDTH), lambda bi, ti: (bi, ti, 0)),
            pl.BlockSpec((nb, POOL_HALO, POOL_WIDTH), lambda bi, ti: (bi, 0, 0)),
            const(GMLP_HEADS, CHUNK, CHUNK),
            const(CHUNK, GMLP_HEADS),
            const(len(POOL_WINDOWS), POOL_CH, POOL_CH),
            const(1, POOL_WIDTH),
        ],
        out_specs=pl.BlockSpec((nb, tr, GMLP_WIDTH + POOL_WIDTH), lambda bi, ti: (bi, ti, 0)),
        out_shape=jax.ShapeDtypeStruct((b, t, GMLP_WIDTH + POOL_WIDTH), BF16),
        scratch_shapes=[pltpu.VMEM((nb, POOL_HALO + tr, POOL_WIDTH), F32)],
        compiler_params=_params(("arbitrary", "arbitrary")),
        name="mix",
    )(rest3, pool_prev, gmlp_ws, gmlp_bt, pool_w, pool_scale)


def _outproj_kernel(x_ref, sb_ref, mix_ref, w_ref, o_ref):
    o_ref[...] = (x_ref[...]
                  + jnp.dot(sb_ref[...], w_ref[0:SB_WIDTH, :], preferred_element_type=F32)
                  + jnp.dot(mix_ref[...], w_ref[SB_WIDTH:, :], preferred_element_type=F32))


def _outproj(x2, o_sb, o_mix, w_out, layer, tm):
    r, d = x2.shape
    return pl.pallas_call(
        _outproj_kernel,
        grid=(r // tm,),
        in_specs=[
            pl.BlockSpec((tm, d), lambda i: (i, 0)),
            pl.BlockSpec((tm, SB_WIDTH), lambda i: (i, 0)),
            pl.BlockSpec((tm, GMLP_WIDTH + POOL_WIDTH), lambda i: (i, 0)),
            pl.BlockSpec((None,) + w_out.shape[1:], lambda i: (layer, 0, 0)),
        ],
        out_specs=pl.BlockSpec((tm, d), lambda i: (i, 0)),
        out_shape=jax.ShapeDtypeStruct((r, d), F32),
        compiler_params=_params(("arbitrary",)),
        name="outproj",
    )(x2, o_sb, o_mix, w_out)


def _ffn_kernel(x_ref, nw_ref, wa_ref, wg_ref, cwa_ref, cwg_ref, cba_ref, cbg_ref, wd_ref,
                pa_ref, pg_ref, y_ref, csa_ref, csg_ref,
                xn_ref, bufa_ref, bufg_ref, haloa_ref, halog_ref, *, nb, tr):
    bi = pl.program_id(0)
    ti = pl.program_id(1)
    j = pl.program_id(2)
    n_j = pl.num_programs(2) - 1
    d = x_ref.shape[-1]
    tn = wa_ref.shape[-1]
    halves = ((cwa_ref, cba_ref, pa_ref, csa_ref, bufa_ref, haloa_ref),
              (cwg_ref, cbg_ref, pg_ref, csg_ref, bufg_ref, halog_ref))

    def up_tile():
        xn = xn_ref[...]
        return tuple(jnp.dot(xn, w_ref[...], preferred_element_type=F32).reshape(nb, tr, tn)
                     for w_ref in (wa_ref, wg_ref))

    def park(ups):
        bufa_ref[:, CONV_HALO:] = ups[0]
        bufg_ref[:, CONV_HALO:] = ups[1]

    def down_tile():
        jj = j - 1
        outs = []
        for cw_ref, cb_ref, prev_ref, cs_ref, buf_ref, halo_ref in halves:
            buf_ref[:, 0:CONV_HALO] = jnp.where(ti == 0, prev_ref[...], halo_ref[jj])
            last = buf_ref[:, tr:tr + CONV_HALO]
            halo_ref[jj] = last
            cs_ref[:, 0] = last
            out = cb_ref[...]
            for i in range(CONV_WIDTH):
                lo = CONV_HALO - (CONV_WIDTH - 1) + i
                out = out + cw_ref[i:i + 1, :] * buf_ref[:, lo:lo + tr]
            outs.append(out.reshape(nb * tr, tn))
        a, g = outs
        h = (jax.nn.silu(g) * a).astype(BF16)
        return jnp.dot(h, wd_ref[...], preferred_element_type=F32).reshape(nb, tr, d)

    @pl.when((bi == 0) & (ti == 0) & (j == 0))
    def _():
        haloa_ref[...] = jnp.zeros_like(haloa_ref)
        halog_ref[...] = jnp.zeros_like(halog_ref)

    @pl.when(j == 0)
    def _():
        xn_ref[...] = _rms_bf16(x_ref[...].reshape(nb * tr, d), nw_ref[...])
        park(up_tile())

    @pl.when(j == 1)
    def _():
        ups = up_tile()
        y_ref[...] = down_tile()
        park(ups)

    @pl.when((j > 1) & (j < n_j))
    def _():
        ups = up_tile()
        y_ref[...] += down_tile()
        park(ups)

    @pl.when(j == n_j)
    def _():
        y_ref[...] = x_ref[...] + (y_ref[...] + down_tile())


def _ffn(x3, nw, w_up, conv_w, conv_b, w_down, prev_a, prev_g, layer, nb, tr):
    b, t, d = x3.shape
    npad = w_up.shape[3]
    n_j = npad // FF_TN
    n_t = t // tr
    assert b % nb == 0 and t % tr == 0 and tr >= CONV_HALO and npad % FF_TN == 0 and n_j >= 2
    up_j = lambda j: jnp.minimum(j, n_j - 1)
    dn_j = lambda j: jnp.maximum(j - 1, 0)
    up = lambda half: pl.BlockSpec((None, None, d, FF_TN), lambda bi, ti, j: (layer, half, 0, up_j(j)))
    col = lambda rows, half: pl.BlockSpec((None, None, rows, FF_TN), lambda bi, ti, j: (layer, half, 0, dn_j(j)))
    halo = pl.BlockSpec((nb, CONV_HALO, FF_TN), lambda bi, ti, j: (bi, 0, dn_j(j)))
    last = pl.BlockSpec((nb, 1, CONV_HALO, FF_TN), lambda bi, ti, j: (bi, ti, 0, dn_j(j)))
    xs = pl.BlockSpec((nb, tr, d), lambda bi, ti, j: (bi, ti, 0))
    x_in = pl.BlockSpec((nb, tr, d), lambda bi, ti, j: (bi, ti, 0), pipeline_mode=pl.Buffered(1))
    return pl.pallas_call(
        functools.partial(_ffn_kernel, nb=nb, tr=tr),
        grid=(b // nb, n_t, n_j + 1),
        in_specs=[
            x_in,
            pl.BlockSpec((None, 1, d), lambda bi, ti, j: (layer, 0, 0)),
            up(0), up(1),
            col(CONV_WIDTH, 0), col(CONV_WIDTH, 1), col(1, 0), col(1, 1),
            pl.BlockSpec((None, FF_TN, d), lambda bi, ti, j: (layer, dn_j(j), 0)),
            halo, halo,
        ],
        out_specs=[xs, last, last],
        out_shape=[
            jax.ShapeDtypeStruct((b, t, d), F32),
            jax.ShapeDtypeStruct((b, n_t, CONV_HALO, npad), F32),
            jax.ShapeDtypeStruct((b, n_t, CONV_HALO, npad), F32),
        ],
        scratch_shapes=[
            pltpu.VMEM((nb * tr, d), BF16),
            pltpu.VMEM((nb, CONV_HALO + tr, FF_TN), F32),
            pltpu.VMEM((nb, CONV_HALO + tr, FF_TN), F32),
            pltpu.VMEM((n_j, nb, CONV_HALO, FF_TN), F32),
            pltpu.VMEM((n_j, nb, CONV_HALO, FF_TN), F32),
        ],
        compiler_params=_params(("arbitrary", "arbitrary", "arbitrary")),
        name="ffn",
    )(x3, nw, w_up, w_up, conv_w, conv_w, conv_b, conv_b, w_down, prev_a, prev_g)


def _ple_kernel(x_ref, p_ref, nw_ref, wg_ref, wp_ref, *refs, final_norm):
    x = x_ref[...]
    gate = jax.nn.sigmoid(jnp.dot(_rms_bf16(x, nw_ref[...]), wg_ref[...], preferred_element_type=F32))
    emb = jnp.dot(p_ref[...].astype(BF16), wp_ref[...], preferred_element_type=F32)
    y = x + emb * gate
    if final_norm:
        fw_ref, o_ref = refs
        ms = jnp.mean(y * y, axis=-1, keepdims=True)
        o_ref[...] = (y * lax.rsqrt(ms + EPS)) * fw_ref[...]
    else:
        refs[0][...] = y


def _ple(x2, p, nw, w_gate, w_ple, layer, tm, final_nw=None):
    r, d = x2.shape
    whole = lambda a: pl.BlockSpec((None,) + a.shape[1:], lambda i: (layer, 0, 0))
    in_specs = [
        pl.BlockSpec((tm, d), lambda i: (i, 0)),
        pl.BlockSpec((None, tm, p.shape[2]), lambda i: (layer, i, 0)),
        whole(nw), whole(w_gate), whole(w_ple),
    ]
    operands = (x2, p, nw, w_gate, w_ple)
    if final_nw is not None:
        in_specs.append(pl.BlockSpec((1, d), lambda i: (0, 0)))
        operands += (final_nw,)
    return pl.pallas_call(
        functools.partial(_ple_kernel, final_norm=final_nw is not None),
        grid=(r // tm,),
        in_specs=in_specs,
        out_specs=pl.BlockSpec((tm, d), lambda i: (i, 0)),
        out_shape=jax.ShapeDtypeStruct((r, d), F32),
        compiler_params=_params(("arbitrary",)),
        name="ple",
    )(*operands)


def _tile(n, cap):
    t = min(n, cap)
    while n % t:
        t //= 2
    return t


def _layer(x3, p, w, layer, pool_prev, conv_prev_a, conv_prev_g, pos0, attn_fn, kv_stack=None, final_nw=None):
    b, t, d = x3.shape
    r = b * t
    one_tile = t < CHUNK
    nb, tr = (b, t) if one_tile else (1, _tile(t, MIX_ROWS))
    tm = r if one_tile else _tile(r, INPROJ_ROWS)

    *kv, qkv, rest = _inproj(x3.reshape(r, d), w["attn_norm_w"], w["w_in"], layer, tm, kv_stack)
    o_sb = attn_fn(qkv.reshape(3, b, t, SB_WIDTH))
    o_mix = _mix(rest.reshape(b, t, REST_WIDTH), pool_prev, w["gmlp_ws"], w["gmlp_bt"],
                 w["pool_w"], w["pool_scale"], layer, nb, tr, pos0)
    tm_res = r if one_tile else _tile(r, RESIDUAL_ROWS)
    x2 = _outproj(x3.reshape(r, d), o_sb.reshape(r, SB_WIDTH), o_mix.reshape(r, -1), w["w_out"], layer, tm_res)
    tr_ffn = tr if one_tile else _tile(t, FFN_ROWS)
    x3, cs_a, cs_g = _ffn(x2.reshape(b, t, d), w["ffn_norm_w"], w["w_up"], w["conv_w"], w["conv_b"], w["w_down"],
                          conv_prev_a, conv_prev_g, layer, nb, tr_ffn)
    x2 = _ple(x3.reshape(r, d), p, w["ple_norm_w"], w["w_ple_gate"], w["w_ple"], layer, tm_res, final_nw)
    return x2.reshape(b, t, d), kv, rest, cs_a, cs_g


def kernel(x_prompt, x_sample, cache_k, cache_v, state_pool, state_conv, page_table, p_prompt, p_sample,
           attn_norm_w, w_in, sb_bias, gmlp_ws, gmlp_b, pool_w, pool_scale, w_out, ffn_norm_w, w_up,
           conv_w, conv_b, w_down, ple_norm_w, w_ple, w_ple_gate, final_norm_w):
    depth = w_in.shape[0]
    bp, tp, d = x_prompt.shape
    bs, ts, _ = x_sample.shape
    n_pages = page_table.shape[1]
    page = cache_k.shape[2]
    past = n_pages * page
    d_ff = w_down.shape[1]
    ff_pad = -(-d_ff // FF_TN) * FF_TN

    cache_k = cache_k.reshape(depth, cache_k.shape[1], page * SB_HEADS, HEAD_DIM)
    cache_v = cache_v.reshape(depth, cache_v.shape[1], page * SB_HEADS, HEAD_DIM)

    def halves(a):
        a = a.reshape(a.shape[:-1] + (2, d_ff))
        return jnp.pad(a, [(0, 0)] * (a.ndim - 1) + [(0, ff_pad - d_ff)])

    w = {
        "attn_norm_w": attn_norm_w[:, None], "ffn_norm_w": ffn_norm_w[:, None], "ple_norm_w": ple_norm_w[:, None],
        "w_in": _to_bf16(w_in, CAST_ROWS),
        "w_out": _to_bf16(w_out, CAST_ROWS),
        "w_ple_gate": _to_bf16(w_ple_gate, CAST_ROWS),
        "w_ple": _to_bf16(w_ple, CAST_ROWS),
        "w_up": _to_bf16(w_up, CAST_ROWS, n_split=2, out_cols=ff_pad),
        "w_down": _to_bf16(w_down, FF_TN, out_rows=ff_pad),
        "conv_w": jnp.swapaxes(halves(conv_w), 1, 2),
        "conv_b": halves(conv_b)[:, :, None],
        "gmlp_ws": gmlp_ws, "gmlp_bt": jnp.swapaxes(gmlp_b, 1, 2), "pool_w": pool_w, "pool_scale": pool_scale[:, None],
    }
    p_prompt = p_prompt.reshape(depth, bp * tp, -1)
    p_sample = p_sample.reshape(depth, bs * ts, -1)
    pool_prev_s = jnp.pad(state_pool, ((0, 0), (0, 0), (POOL_HALO - POOL_STATE, 0), (0, 0)))
    conv_prev_s = halves(jnp.pad(state_conv, ((0, 0), (0, 0), (CONV_HALO - (CONV_WIDTH - 1), 0), (0, 0))))
    pool_prev_p = jnp.zeros((bp, POOL_HALO, POOL_WIDTH), F32)
    conv_prev_p = jnp.zeros((bp, CONV_HALO, ff_pad), F32)

    xp_, xs_ = x_prompt, x_sample
    outs = {k: [] for k in ("ks", "vs", "poolp", "pools", "convp", "convs", "gvs")}
    kv_prompt = [jnp.zeros((depth, bp * tp * SB_HEADS, HEAD_DIM), F32) for _ in range(2)]
    for l in range(depth):
        final_nw = final_norm_w[None] if l == depth - 1 else None
        attn_p = functools.partial(_attn_prompt, sb_bias=sb_bias, layer=l, blk=_tile(tp, ATTN_BLOCK))
        xp_, kv_prompt, rest_p, csa_p, csg_p = _layer(
            xp_, p_prompt, w, l, pool_prev_p, conv_prev_p, conv_prev_p, 0, attn_p, kv_stack=kv_prompt,
            final_nw=final_nw)
        nbias_col = jnp.repeat(sb_bias[l], ts)[:, None] * (-LOG2E)

        def attn_s(qkv4):
            return _attn_sample(qkv4[0], qkv4[1], qkv4[2], nbias_col, cache_k, cache_v, page_table, l,
                                n_pp=_tile(n_pages, SAMPLE_PAGES_PER_STEP))

        xs_, (kv_s,), rest_s, csa_s, csg_s = _layer(
            xs_, p_sample, w, l, pool_prev_s[l], conv_prev_s[l, :, :, 0], conv_prev_s[l, :, :, 1], past, attn_s,
            final_nw=final_nw)

        outs["ks"].append(kv_s[0].reshape(bs, ts, SB_HEADS, HEAD_DIM))
        outs["vs"].append(kv_s[1].reshape(bs, ts, SB_HEADS, HEAD_DIM))
        xpool_p = rest_p.reshape(bp, tp, REST_WIDTH)[:, :, 2 * GMLP_WIDTH:]
        xpool_s = rest_s.reshape(bs, ts, REST_WIDTH)[:, :, 2 * GMLP_WIDTH:]
        outs["poolp"].append(xpool_p[:, -POOL_STATE:])
        outs["pools"].append(jnp.concatenate([state_pool[l], xpool_s], axis=1)[:, -POOL_STATE:])
        tail = slice(CONV_HALO - (CONV_WIDTH - 1), CONV_HALO)
        outs["convp"].append(jnp.concatenate([csa_p[:, -1, tail, :d_ff], csg_p[:, -1, tail, :d_ff]], axis=-1))
        outs["convs"].append(jnp.concatenate([csa_s[:, -1, tail, :d_ff], csg_s[:, -1, tail, :d_ff]], axis=-1))
        outs["gvs"].append(rest_s.reshape(bs, ts, REST_WIDTH)[:, :, GMLP_WIDTH:2 * GMLP_WIDTH])

    k_prompt, v_prompt = (a.reshape(depth, bp, tp, SB_HEADS, HEAD_DIM) for a in kv_prompt)
    return (xp_, xs_,
            k_prompt, v_prompt, jnp.stack(outs["ks"]), jnp.stack(outs["vs"]),
            jnp.stack(outs["poolp"]), jnp.stack(outs["pools"]),
            jnp.stack(outs["convp"]), jnp.stack(outs["convs"]), jnp.stack(outs["gvs"]))
```

```python
import functools

import jax
import jax.numpy as jnp
from jax import lax
from jax.experimental import pallas as pl
from jax.experimental.pallas import tpu as pltpu

F32 = jnp.float32
BF16 = jnp.bfloat16

EPS = 1e-6
HEAD_DIM = 128
SB_HEADS = 8
SB_WIDTH = SB_HEADS * HEAD_DIM
GMLP_HEADS = 4
CHUNK = 128
GMLP_WIDTH = GMLP_HEADS * CHUNK
POOL_WINDOWS = (2, 4, 8, 16)
POOL_CH = 128
POOL_WIDTH = POOL_CH * len(POOL_WINDOWS)
POOL_STATE = max(POOL_WINDOWS) - 1
POOL_HALO = 16
CONV_WIDTH = 3
CONV_HALO = 8
REST_WIDTH = 2 * GMLP_WIDTH + POOL_WIDTH
SCALE = HEAD_DIM ** -0.5
LOG2E = 1.4426950408889634

IN_TN = 512
FF_TN = 512
FF_SUB = 256
CAST_ROWS = 256
SAMPLE_PAGES_PER_STEP = 16
ATTN_BLOCK = 256
INPROJ_ROWS = 1024
MIX_ROWS = 512
RESIDUAL_ROWS = 512
FFN_ROWS = 1024
F32_SIGN_BIT = 0x80000000
V7X_VMEM_LIMIT = 56 * 1024 * 1024


def _params(sem):
    return pltpu.CompilerParams(dimension_semantics=sem, vmem_limit_bytes=V7X_VMEM_LIMIT)


def _rms_bf16(x, w):
    ms = jnp.mean(x * x, axis=-1, keepdims=True)
    return ((x * lax.rsqrt(ms + EPS)) * w).astype(BF16)


def _cast_kernel(x_ref, o_ref, *, rows_valid):
    x = x_ref[...]
    tr, c = x.shape
    if rows_valid is not None:
        r = pl.program_id(1) * tr + lax.broadcasted_iota(jnp.int32, (tr, 1), 0)
        x = jnp.where(r < rows_valid, x, 0.0)
    o_ref[:, 0:c] = x.astype(BF16)
    if o_ref.shape[1] > c:
        o_ref[:, c:] = jnp.zeros((tr, o_ref.shape[1] - c), BF16)


def _to_bf16(w, tr, n_split=1, out_rows=None, out_cols=None):
    n_l, r, c_all = w.shape
    c = c_all // n_split
    out_rows = out_rows or r
    out_cols = out_cols or c
    assert c_all == n_split * c and out_rows % tr == 0 and out_rows - r < tr and out_cols >= c
    out = pl.pallas_call(
        functools.partial(_cast_kernel, rows_valid=r if out_rows > r else None),
        grid=(n_l, out_rows // tr, n_split),
        in_specs=[pl.BlockSpec((None, tr, c), lambda l, i, s: (l, i, s))],
        out_specs=pl.BlockSpec((None, None, tr, out_cols), lambda l, i, s: (l, s, i, 0)),
        out_shape=jax.ShapeDtypeStruct((n_l, n_split, out_rows, out_cols), BF16),
        compiler_params=_params(("arbitrary", "arbitrary", "arbitrary")),
        name="to_bf16",
    )(w)
    return out if n_split > 1 else out.reshape(n_l, out_rows, out_cols)


def _inproj_kernel(x_ref, nw_ref, w_ref, *refs, stacked_kv):
    if stacked_kv:
        k_ref, v_ref, qkvb_ref, rest_ref, xn_ref = refs[2:]
    else:
        kv_ref, qkvb_ref, rest_ref, xn_ref = refs
    j = pl.program_id(1)
    tm = x_ref.shape[0]
    heads_per_tile = IN_TN // HEAD_DIM

    @pl.when(j == 0)
    def _():
        xn_ref[...] = _rms_bf16(x_ref[...], nw_ref[...])

    def tile():
        return jnp.dot(xn_ref[...], w_ref[...], preferred_element_type=F32)

    @pl.when(j < 2)
    def _():
        qkvb_ref[0] = tile().astype(BF16)

    if stacked_kv:
        for first_tile, ref in ((2, k_ref), (4, v_ref)):
            @pl.when((j == first_tile) | (j == first_tile + 1))
            def _(first_tile=first_tile, ref=ref):
                acc = tile()
                qkvb_ref[0] = acc.astype(BF16)
                first_head = (j - first_tile) * heads_per_tile
                for hh in range(heads_per_tile):
                    ref[pl.ds(first_head + hh, tm, stride=SB_HEADS), :] = acc[:, hh * HEAD_DIM:(hh + 1) * HEAD_DIM]
    else:
        @pl.when((j >= 2) & (j < 6))
        def _():
            acc = tile()
            qkvb_ref[0] = acc.astype(BF16)
            kv_ref[0] = acc

    @pl.when((j == 6) | (j == 7))
    def _():
        rest_ref[...] = jax.nn.gelu(tile())

    @pl.when(j == 8)
    def _():
        rest_ref[...] = tile()


def _inproj(x2, nw, w_in, layer, tm, kv_stack=None):
    r, d = x2.shape
    n_j = w_in.shape[2] // IN_TN
    assert 3 * SB_WIDTH == 6 * IN_TN and n_j == 9 and r % tm == 0

    def kv_map(i, j):
        c = jnp.clip(j - 2, 0, 3)
        return (c // 2, i, c % 2)

    def qkv_map(i, j):
        c = jnp.minimum(j, 5)
        return (c // 2, i, c % 2)

    in_specs = [
        pl.BlockSpec((tm, d), lambda i, j: (i, 0)),
        pl.BlockSpec((None, 1, d), lambda i, j: (layer, 0, 0)),
        pl.BlockSpec((None, d, IN_TN), lambda i, j: (layer, 0, j)),
    ]
    common_specs = [
        pl.BlockSpec((1, tm, IN_TN), qkv_map),
        pl.BlockSpec((tm, IN_TN), lambda i, j: (i, jnp.maximum(j - 6, 0))),
    ]
    common_shapes = [
        jax.ShapeDtypeStruct((3, r, SB_WIDTH), BF16),
        jax.ShapeDtypeStruct((r, REST_WIDTH), F32),
    ]
    if kv_stack is None:
        operands, aliases = (x2, nw, w_in), {}
        kv_specs = [pl.BlockSpec((1, tm, IN_TN), kv_map)]
        kv_shapes = [jax.ShapeDtypeStruct((2, r, SB_WIDTH), F32)]
    else:
        operands, aliases = (x2, nw, w_in) + tuple(kv_stack), {3: 0, 4: 1}
        in_specs += [pl.BlockSpec(memory_space=pl.ANY)] * 2
        kv_specs = [pl.BlockSpec((None, tm * SB_HEADS, HEAD_DIM), lambda i, j: (layer, i, 0))] * 2
        kv_shapes = [jax.ShapeDtypeStruct(a.shape, a.dtype) for a in kv_stack]
    return pl.pallas_call(
        functools.partial(_inproj_kernel, stacked_kv=kv_stack is not None),
        grid=(r // tm, n_j),
        in_specs=in_specs,
        out_specs=kv_specs + common_specs,
        out_shape=kv_shapes + common_shapes,
        input_output_aliases=aliases,
        scratch_shapes=[pltpu.VMEM((tm, d), BF16)],
        compiler_params=_params(("arbitrary", "arbitrary")),
        name="inproj",
    )(*operands)


def _sb_terms(s, nbias2):
    nz2 = s * (-SCALE * LOG2E) + nbias2
    neg_abs = lax.bitcast_convert_type(lax.bitcast_convert_type(nz2, jnp.uint32) | jnp.uint32(F32_SIGN_BIT), F32)
    l2 = jnp.log2(1.0 + jnp.exp2(neg_abs))
    lk2 = jnp.minimum(nz2, 0.0) - l2
    lb2 = lk2 - nz2
    return lb2, lk2


def _suffix_sum(lk, u):
    return jnp.dot(lk.astype(BF16), u, preferred_element_type=F32)


def _later_key_matrix(n):
    row = lax.broadcasted_iota(jnp.int32, (n, n), 0)
    col = lax.broadcasted_iota(jnp.int32, (n, n), 1)
    return row, col, jnp.where(row > col, 1.0, 0.0).astype(BF16)


def _nt_dot(a, b):
    return lax.dot_general(a, b, (((1,), (1,)), ((), ())), preferred_element_type=F32)


def _attn_prompt_kernel(bias_ref, q_ref, k_ref, v_ref, o_ref, *, blk, layer):
    qi = pl.program_id(1)
    row, col, u = _later_key_matrix(blk)
    vis = col < row
    heads = [slice(h * HEAD_DIM, (h + 1) * HEAD_DIM) for h in range(SB_HEADS)]
    nbias2 = [bias_ref[layer, h] * (-LOG2E) for h in range(SB_HEADS)]

    def block(kb, state, masked):
        keys = pl.ds(pl.multiple_of(kb * blk, blk), blk)
        new_state = [None] * SB_HEADS

        def scores(h):
            return _nt_dot(q_ref[0, 0, :, heads[h]], k_ref[0, 0, keys, heads[h]])

        def finish(h, lb, lk, after):
            carry, acc = state[h]
            w = jnp.exp2(lb + after + carry)
            if masked:
                w = jnp.where(vis, w, 0.0)
            acc = acc + jnp.dot(w.astype(BF16), v_ref[0, 0, keys, heads[h]], preferred_element_type=F32)
            new_state[h] = (carry + after[:, 0:1] + lk[:, 0:1], acc)

        s_next = scores(0)
        pending = None
        for h in range(SB_HEADS):
            s, s_next = s_next, (scores(h + 1) if h + 1 < SB_HEADS else None)
            lb, lk = _sb_terms(s, nbias2[h])
            if masked:
                lk = jnp.where(vis, lk, 0.0)
            after = _suffix_sum(lk, u)
            if pending is not None:
                finish(*pending)
            pending = (h, lb, lk, after)
        finish(*pending)
        return tuple(new_state)

    state = tuple((jnp.zeros((blk, 1), F32), jnp.zeros((blk, HEAD_DIM), F32)) for _ in heads)
    state = block(qi, state, True)
    state = lax.fori_loop(0, qi, lambda it, st: block(qi - 1 - it, st, False), state)
    for h, sl in enumerate(heads):
        o_ref[0, :, sl] = state[h][1].astype(o_ref.dtype)


def _attn_prompt(qkv4, sb_bias, layer, blk):
    _, b, t, _ = qkv4.shape
    assert t % blk == 0
    return pl.pallas_call(
        functools.partial(_attn_prompt_kernel, blk=blk, layer=layer),
        grid=(b, t // blk),
        in_specs=[
            pl.BlockSpec(memory_space=pltpu.SMEM),
            pl.BlockSpec((1, 1, blk, SB_WIDTH), lambda bi, qi: (0, bi, qi, 0)),
            pl.BlockSpec((1, 1, t, SB_WIDTH), lambda bi, qi: (1, bi, 0, 0)),
            pl.BlockSpec((1, 1, t, SB_WIDTH), lambda bi, qi: (2, bi, 0, 0)),
        ],
        out_specs=pl.BlockSpec((1, blk, SB_WIDTH), lambda bi, qi: (bi, qi, 0)),
        out_shape=jax.ShapeDtypeStruct((b, t, SB_WIDTH), BF16),
        compiler_params=_params(("arbitrary", "arbitrary")),
        name="attn_prompt",
    )(sb_bias, qkv4, qkv4, qkv4)


def _attn_sample_kernel(pt_ref, q_ref, kn_ref, vn_ref, nb_ref, *refs, n_pp, t_new):
    k_refs = refs[:n_pp]
    v_refs = refs[n_pp:2 * n_pp]
    o_ref = refs[2 * n_pp]
    qbd_ref, acc_ref, carry_ref = refs[2 * n_pp + 1:]
    step = pl.program_id(1)
    n_rows = SB_HEADS * t_new
    page = k_refs[0].shape[0] // SB_HEADS
    _, _, u = _later_key_matrix(page)
    nbias = nb_ref[...]

    def blocks(ks, vs, vis):
        qbd = qbd_ref[...]
        terms = [_sb_terms(_nt_dot(qbd, k), nbias) for k in ks]
        if vis is not None:
            terms = [(lb, jnp.where(vis, lk, 0.0)) for lb, lk in terms]
        after = [_suffix_sum(lk, u) for _, lk in terms]
        carry = carry_ref[...]
        acc = acc_ref[...]
        for (lb, lk), af, v in zip(terms, after, vs):
            w = jnp.exp2(lb + af + carry)
            if vis is not None:
                w = jnp.where(vis, w, 0.0)
            acc = acc + jnp.dot(w.astype(BF16), v, preferred_element_type=F32)
            carry = carry + af[:, 0:1] + lk[:, 0:1]
        carry_ref[...] = carry
        acc_ref[...] = acc

    @pl.when(step == 0)
    def _():
        q = q_ref[0].astype(F32)
        qt = jnp.concatenate([q] * SB_HEADS, axis=0)
        rh = lax.broadcasted_iota(jnp.int32, (n_rows, SB_WIDTH), 0) // t_new
        ch = lax.broadcasted_iota(jnp.int32, (n_rows, SB_WIDTH), 1) // HEAD_DIM
        qbd_ref[...] = jnp.where(rh == ch, qt, 0.0).astype(BF16)
        acc_ref[...] = jnp.zeros_like(acc_ref)
        carry_ref[...] = jnp.zeros_like(carry_ref)
        pad = jnp.zeros((page - t_new, SB_WIDTH), F32)
        kn = jnp.concatenate([kn_ref[0].astype(F32), pad], axis=0).astype(BF16)
        vn = jnp.concatenate([vn_ref[0].astype(F32), pad], axis=0).astype(BF16)
        rt = lax.broadcasted_iota(jnp.int32, (n_rows, page), 0) % t_new
        ck = lax.broadcasted_iota(jnp.int32, (n_rows, page), 1)
        blocks([kn], [vn], ck < rt)

    def heads_on_lanes(ref):
        return jnp.concatenate(
            [ref[pl.ds(h, page, stride=SB_HEADS), :].astype(BF16) for h in range(SB_HEADS)], axis=1)

    blocks([heads_on_lanes(r) for r in k_refs], [heads_on_lanes(r) for r in v_refs], None)

    @pl.when(step == pl.num_programs(1) - 1)
    def _():
        for h in range(SB_HEADS):
            o_ref[0, :, h * HEAD_DIM:(h + 1) * HEAD_DIM] = acc_ref[
                h * t_new:(h + 1) * t_new, h * HEAD_DIM:(h + 1) * HEAD_DIM].astype(o_ref.dtype)


def _attn_sample(q3, kn3, vn3, nbias_col, cache_k, cache_v, page_table, layer, n_pp):
    b, t_new, _ = q3.shape
    n_pages = page_table.shape[1]
    page = cache_k.shape[2] // SB_HEADS
    assert n_pages % n_pp == 0 and t_new <= page
    n_rows = SB_HEADS * t_new

    def kv_spec(p):
        return pl.BlockSpec(
            (None, None, page * SB_HEADS, HEAD_DIM),
            lambda bi, s, pt: (layer, pt[bi, n_pages - 1 - (s * n_pp + p)], 0, 0))

    row_spec = pl.BlockSpec((1, t_new, SB_WIDTH), lambda bi, s, pt: (bi, 0, 0))
    grid_spec = pltpu.PrefetchScalarGridSpec(
        num_scalar_prefetch=1,
        grid=(b, n_pages // n_pp),
        in_specs=[row_spec, row_spec, row_spec,
                  pl.BlockSpec((n_rows, 1), lambda bi, s, pt: (0, 0))]
        + [kv_spec(p) for p in range(n_pp)] + [kv_spec(p) for p in range(n_pp)],
        out_specs=row_spec,
        scratch_shapes=[pltpu.VMEM((n_rows, SB_WIDTH), BF16),
                        pltpu.VMEM((n_rows, SB_WIDTH), F32),
                        pltpu.VMEM((n_rows, 1), F32)],
    )
    return pl.pallas_call(
        functools.partial(_attn_sample_kernel, n_pp=n_pp, t_new=t_new),
        grid_spec=grid_spec,
        out_shape=jax.ShapeDtypeStruct((b, t_new, SB_WIDTH), BF16),
        compiler_params=_params(("arbitrary", "arbitrary")),
        name="attn_sample",
    )(page_table, q3, kn3, vn3, nbias_col, *([cache_k] * n_pp), *([cache_v] * n_pp))


def _mix_kernel(rest_ref, prev_ref, ws_ref, gbt_ref, pw_ref, ps_ref, o_ref, pbuf_ref, *, nb, tr, pos0):
    ti = pl.program_id(1)

    @pl.when(ti == 0)
    def _():
        pbuf_ref[:, 0:POOL_HALO] = prev_ref[...]

    @pl.when(ti > 0)
    def _():
        pbuf_ref[:, 0:POOL_HALO] = pbuf_ref[:, tr:tr + POOL_HALO]

    pbuf_ref[:, POOL_HALO:] = rest_ref[:, :, 2 * GMLP_WIDTH:]

    tc = min(tr, CHUNK)
    row, col, _ = _later_key_matrix(CHUNK)
    causal = row >= col
    pos = (pos0 + ti * tr + lax.broadcasted_iota(jnp.int32, (tr, 1), 0) + 1).astype(F32)

    wsm_heads = [jnp.where(causal, ws_ref[g], 0.0).astype(BF16) for g in range(GMLP_HEADS)]

    for bb in range(nb):
        for g in range(GMLP_HEADS):
            sl = slice(g * CHUNK, (g + 1) * CHUNK)
            wsm = wsm_heads[g]
            bias = gbt_ref[0:tc, g:g + 1]
            for c in range(tr // tc):
                rs = slice(c * tc, (c + 1) * tc)
                u = rest_ref[bb, rs, sl]
                v = rest_ref[bb, rs, GMLP_WIDTH + g * CHUNK:GMLP_WIDTH + (g + 1) * CHUNK]
                if tc < CHUNK:
                    v = jnp.concatenate([v, jnp.zeros((CHUNK - tc, CHUNK), F32)], axis=0)
                mixed = jnp.dot(wsm, v.astype(BF16), preferred_element_type=F32)[0:tc] + bias
                o_ref[bb, rs, sl] = (u * mixed).astype(o_ref.dtype)
        for g, win in enumerate(POOL_WINDOWS):
            sl = slice(g * POOL_CH, (g + 1) * POOL_CH)
            x = pbuf_ref[bb, POOL_HALO:POOL_HALO + tr, sl]
            tot = x
            for d in range(1, win):
                tot = tot + pbuf_ref[bb, POOL_HALO - d:POOL_HALO - d + tr, sl]
            pooled = tot / jnp.minimum(pos, float(win)) - x
            mapped = jnp.dot(pooled.astype(BF16), pw_ref[g].astype(BF16), preferred_element_type=F32)
            o_ref[bb, :, GMLP_WIDTH + g * POOL_CH:GMLP_WIDTH + (g + 1) * POOL_CH] = (
                mapped * ps_ref[:, sl]).astype(o_ref.dtype)


def _mix(rest3, pool_prev, gmlp_ws, gmlp_bt, pool_w, pool_scale, layer, nb, tr, pos0):
    b, t, _ = rest3.shape
    assert b % nb == 0 and t % tr == 0 and (tr % CHUNK == 0 or (tr < CHUNK and t == tr))
    const = lambda *shape: pl.BlockSpec((None,) + shape, lambda bi, ti: (layer,) + (0,) * len(shape))
    return pl.pallas_call(
        functools.partial(_mix_kernel, nb=nb, tr=tr, pos0=pos0),
        grid=(b // nb, t // tr),
        in_specs=[
            pl.BlockSpec((nb, tr, REST_WIDTH), lambda bi, ti: (bi, ti, 0)),
            pl.BlockSpec((nb, POOL_HALO, POOL_WIDTH), lambda bi, ti: (bi, 0, 0)),
            const(GMLP_HEADS, CHUNK, CHUNK),
            const(CHUNK, GMLP_HEADS),
            const(len(POOL_WINDOWS), POOL_CH, POOL_CH),
            const(1, POOL_WIDTH),
        ],
        out_specs=pl.BlockSpec((nb, tr, GMLP_WIDTH + POOL_WIDTH), lambda bi, ti: (bi, ti, 0)),
        out_shape=jax.ShapeDtypeStruct((b, t, GMLP_WIDTH + POOL_WIDTH), BF16),
        scratch_shapes=[pltpu.VMEM((nb, POOL_HALO + tr, POOL_WIDTH), F32)],
        compiler_params=_params(("arbitrary", "arbitrary")),
        name="mix",
    )(rest3, pool_prev, gmlp_ws, gmlp_bt, pool_w, pool_scale)


def _outproj_kernel(x_ref, sb_ref, mix_ref, w_ref, o_ref):
    o_ref[...] = (x_ref[...]
                  + jnp.dot(sb_ref[...], w_ref[0:SB_WIDTH, :], preferred_element_type=F32)
                  + jnp.dot(mix_ref[...], w_ref[SB_WIDTH:, :], preferred_element_type=F32))


def _outproj(x2, o_sb, o_mix, w_out, layer, tm):
    r, d = x2.shape
    return pl.pallas_call(
        _outproj_kernel,
        grid=(r // tm,),
        in_specs=[
            pl.BlockSpec((tm, d), lambda i: (i, 0)),
            pl.BlockSpec((tm, SB_WIDTH), lambda i: (i, 0)),
            pl.BlockSpec((tm, GMLP_WIDTH + POOL_WIDTH), lambda i: (i, 0)),
            pl.BlockSpec((None,) + w_out.shape[1:], lambda i: (layer, 0, 0)),
        ],
        out_specs=pl.BlockSpec((tm, d), lambda i: (i, 0)),
        out_shape=jax.ShapeDtypeStruct((r, d), F32),
        compiler_params=_params(("arbitrary",)),
        name="outproj",
    )(x2, o_sb, o_mix, w_out)


def _ffn_kernel(x_ref, nw_ref, wa_ref, wg_ref, cwa_ref, cwg_ref, cba_ref, cbg_ref, wd_ref,
                pa_ref, pg_ref, y_ref, csa_ref, csg_ref,
                xn_ref, bufa_ref, bufg_ref, haloa_ref, halog_ref, *, nb, tr):
    bi = pl.program_id(0)
    ti = pl.program_id(1)
    j = pl.program_id(2)
    n_j = pl.num_programs(2) - 1
    d = x_ref.shape[-1]
    tn = wa_ref.shape[-1]
    halves = ((wa_ref, cwa_ref, cba_ref, pa_ref, csa_ref, bufa_ref, haloa_ref),
              (wg_ref, cwg_ref, cbg_ref, pg_ref, csg_ref, bufg_ref, halog_ref))
    cols = [slice(c * FF_SUB, (c + 1) * FF_SUB) for c in range(tn // FF_SUB)]

    def up(half):
        w_ref = halves[half][0]
        return [jnp.dot(xn_ref[...], w_ref[:, cs], preferred_element_type=F32).reshape(nb, tr, FF_SUB)
                for cs in cols]

    def park(half, ups):
        buf_ref = halves[half][5]
        for cs, u in zip(cols, ups):
            buf_ref[:, CONV_HALO:, cs] = u

    def conv(half):
        _, cw_ref, cb_ref, prev_ref, cs_ref, buf_ref, halo_ref = halves[half]
        jj = j - 1
        outs = []
        for cs in cols:
            buf_ref[:, 0:CONV_HALO, cs] = jnp.where(ti == 0, prev_ref[:, :, cs], halo_ref[jj, :, :, cs])
            last = buf_ref[:, tr:tr + CONV_HALO, cs]
            halo_ref[jj, :, :, cs] = last
            cs_ref[:, 0, :, cs] = last
            out = cb_ref[:, cs]
            for i in range(CONV_WIDTH):
                lo = CONV_HALO - (CONV_WIDTH - 1) + i
                out = out + cw_ref[i:i + 1, cs] * buf_ref[:, lo:lo + tr, cs]
            outs.append(out.reshape(nb * tr, FF_SUB))
        return outs

    def down(conv_a, conv_g):
        tot = None
        for cs, a, g in zip(cols, conv_a, conv_g):
            h = (jax.nn.silu(g) * a).astype(BF16)
            part = jnp.dot(h, wd_ref[cs, :], preferred_element_type=F32)
            tot = part if tot is None else tot + part
        return tot.reshape(nb, tr, d)

    def step(first):
        up_a = up(0)
        conv_a, conv_g = conv(0), conv(1)
        up_g = up(1)
        part = down(conv_a, conv_g)
        if first:
            y_ref[...] = part
        else:
            y_ref[...] += part
        park(0, up_a)
        park(1, up_g)

    @pl.when((bi == 0) & (ti == 0) & (j == 0))
    def _():
        haloa_ref[...] = jnp.zeros_like(haloa_ref)
        halog_ref[...] = jnp.zeros_like(halog_ref)

    @pl.when(j == 0)
    def _():
        xn_ref[...] = _rms_bf16(x_ref[...].reshape(nb * tr, d), nw_ref[...])
        park(0, up(0))
        park(1, up(1))

    @pl.when(j == 1)
    def _():
        step(True)

    @pl.when((j > 1) & (j < n_j))
    def _():
        step(False)

    @pl.when(j == n_j)
    def _():
        y_ref[...] = x_ref[...] + (y_ref[...] + down(conv(0), conv(1)))


def _ffn(x3, nw, w_up, conv_w, conv_b, w_down, prev_a, prev_g, layer, nb, tr):
    b, t, d = x3.shape
    npad = w_up.shape[3]
    n_j = npad // FF_TN
    n_t = t // tr
    assert b % nb == 0 and t % tr == 0 and tr >= CONV_HALO and npad % FF_TN == 0 and n_j >= 2
    up_j = lambda j: jnp.minimum(j, n_j - 1)
    dn_j = lambda j: jnp.maximum(j - 1, 0)
    up = lambda half: pl.BlockSpec((None, None, d, FF_TN), lambda bi, ti, j: (layer, half, 0, up_j(j)))
    col = lambda rows, half: pl.BlockSpec((None, None, rows, FF_TN), lambda bi, ti, j: (layer, half, 0, dn_j(j)))
    halo = pl.BlockSpec((nb, CONV_HALO, FF_TN), lambda bi, ti, j: (bi, 0, dn_j(j)))
    last = pl.BlockSpec((nb, 1, CONV_HALO, FF_TN), lambda bi, ti, j: (bi, ti, 0, dn_j(j)))
    xs = pl.BlockSpec((nb, tr, d), lambda bi, ti, j: (bi, ti, 0))
    x_in = pl.BlockSpec((nb, tr, d), lambda bi, ti, j: (bi, ti, 0), pipeline_mode=pl.Buffered(1))
    return pl.pallas_call(
        functools.partial(_ffn_kernel, nb=nb, tr=tr),
        grid=(b // nb, n_t, n_j + 1),
        in_specs=[
            x_in,
            pl.BlockSpec((None, 1, d), lambda bi, ti, j: (layer, 0, 0)),
            up(0), up(1),
            col(CONV_WIDTH, 0), col(CONV_WIDTH, 1), col(1, 0), col(1, 1),
            pl.BlockSpec((None, FF_TN, d), lambda bi, ti, j: (layer, dn_j(j), 0)),
            halo, halo,
        ],
        out_specs=[xs, last, last],
        out_shape=[
            jax.ShapeDtypeStruct((b, t, d), F32),
            jax.ShapeDtypeStruct((b, n_t, CONV_HALO, npad), F32),
            jax.ShapeDtypeStruct((b, n_t, CONV_HALO, npad), F32),
        ],
        scratch_shapes=[
            pltpu.VMEM((nb * tr, d), BF16),
            pltpu.VMEM((nb, CONV_HALO + tr, FF_TN), F32),
            pltpu.VMEM((nb, CONV_HALO + tr, FF_TN), F32),
            pltpu.VMEM((n_j, nb, CONV_HALO, FF_TN), F32),
            pltpu.VMEM((n_j, nb, CONV_HALO, FF_TN), F32),
        ],
        compiler_params=_params(("arbitrary", "arbitrary", "arbitrary")),
        name="ffn",
    )(x3, nw, w_up, w_up, conv_w, conv_w, conv_b, conv_b, w_down, prev_a, prev_g)


def _ple_kernel(x_ref, p_ref, nw_ref, wg_ref, wp_ref, *refs, final_norm):
    x = x_ref[...]
    gate = jax.nn.sigmoid(jnp.dot(_rms_bf16(x, nw_ref[...]), wg_ref[...], preferred_element_type=F32))
    emb = jnp.dot(p_ref[...].astype(BF16), wp_ref[...], preferred_element_type=F32)
    y = x + emb * gate
    if final_norm:
        fw_ref, o_ref = refs
        ms = jnp.mean(y * y, axis=-1, keepdims=True)
        o_ref[...] = (y * lax.rsqrt(ms + EPS)) * fw_ref[...]
    else:
        refs[0][...] = y


def _ple(x2, p, nw, w_gate, w_ple, layer, tm, final_nw=None):
    r, d = x2.shape
    whole = lambda a: pl.BlockSpec((None,) + a.shape[1:], lambda i: (layer, 0, 0))
    in_specs = [
        pl.BlockSpec((tm, d), lambda i: (i, 0)),
        pl.BlockSpec((None, tm, p.shape[2]), lambda i: (layer, i, 0)),
        whole(nw), whole(w_gate), whole(w_ple),
    ]
    operands = (x2, p, nw, w_gate, w_ple)
    if final_nw is not None:
        in_specs.append(pl.BlockSpec((1, d), lambda i: (0, 0)))
        operands += (final_nw,)
    return pl.pallas_call(
        functools.partial(_ple_kernel, final_norm=final_nw is not None),
        grid=(r // tm,),
        in_specs=in_specs,
        out_specs=pl.BlockSpec((tm, d), lambda i: (i, 0)),
        out_shape=jax.ShapeDtypeStruct((r, d), F32),
        compiler_params=_params(("arbitrary",)),
        name="ple",
    )(*operands)


def _tile(n, cap):
    t = min(n, cap)
    while n % t:
        t //= 2
    return t


def _layer(x3, p, w, layer, pool_prev, conv_prev_a, conv_prev_g, pos0, attn_fn, kv_stack=None, final_nw=None):
    b, t, d = x3.shape
    r = b * t
    one_tile = t < CHUNK
    nb, tr = (b, t) if one_tile else (1, _tile(t, MIX_ROWS))
    tm = r if one_tile else _tile(r, INPROJ_ROWS)

    *kv, qkv, rest = _inproj(x3.reshape(r, d), w["attn_norm_w"], w["w_in"], layer, tm, kv_stack)
    o_sb = attn_fn(qkv.reshape(3, b, t, SB_WIDTH))
    o_mix = _mix(rest.reshape(b, t, REST_WIDTH), pool_prev, w["gmlp_ws"], w["gmlp_bt"],
                 w["pool_w"], w["pool_scale"], layer, nb, tr, pos0)
    tm_res = r if one_tile else _tile(r, RESIDUAL_ROWS)
    x2 = _outproj(x3.reshape(r, d), o_sb.reshape(r, SB_WIDTH), o_mix.reshape(r, -1), w["w_out"], layer, tm_res)
    tr_ffn = tr if one_tile else _tile(t, FFN_ROWS)
    x3, cs_a, cs_g = _ffn(x2.reshape(b, t, d), w["ffn_norm_w"], w["w_up"], w["conv_w"], w["conv_b"], w["w_down"],
                          conv_prev_a, conv_prev_g, layer, nb, tr_ffn)
    x2 = _ple(x3.reshape(r, d), p, w["ple_norm_w"], w["w_ple_gate"], w["w_ple"], layer, tm_res, final_nw)
    return x2.reshape(b, t, d), kv, rest, cs_a, cs_g


def kernel(x_prompt, x_sample, cache_k, cache_v, state_pool, state_conv, page_table, p_prompt, p_sample,
           attn_norm_w, w_in, sb_bias, gmlp_ws, gmlp_b, pool_w, pool_scale, w_out, ffn_norm_w, w_up,
           conv_w, conv_b, w_down, ple_norm_w, w_ple, w_ple_gate, final_norm_w):
    depth = w_in.shape[0]
    bp, tp, d = x_prompt.shape
    bs, ts, _ = x_sample.shape
    n_pages = page_table.shape[1]
    page = cache_k.shape[2]
    past = n_pages * page
    d_ff = w_down.shape[1]
    ff_pad = -(-d_ff // FF_TN) * FF_TN

    cache_k = cache_k.reshape(depth, cache_k.shape[1], page * SB_HEADS, HEAD_DIM)
    cache_v = cache_v.reshape(depth, cache_v.shape[1], page * SB_HEADS, HEAD_DIM)

    def halves(a):
        a = a.reshape(a.shape[:-1] + (2, d_ff))
        return jnp.pad(a, [(0, 0)] * (a.ndim - 1) + [(0, ff_pad - d_ff)])

    w = {
        "attn_norm_w": attn_norm_w[:, None], "ffn_norm_w": ffn_norm_w[:, None], "ple_norm_w": ple_norm_w[:, None],
        "w_in": _to_bf16(w_in, CAST_ROWS),
        "w_out": _to_bf16(w_out, CAST_ROWS),
        "w_ple_gate": _to_bf16(w_ple_gate, CAST_ROWS),
        "w_ple": _to_bf16(w_ple, CAST_ROWS),
        "w_up": _to_bf16(w_up, CAST_ROWS, n_split=2, out_cols=ff_pad),
        "w_down": _to_bf16(w_down, FF_TN, out_rows=ff_pad),
        "conv_w": jnp.swapaxes(halves(conv_w), 1, 2),
        "conv_b": halves(conv_b)[:, :, None],
        "gmlp_ws": gmlp_ws, "gmlp_bt": jnp.swapaxes(gmlp_b, 1, 2), "pool_w": pool_w, "pool_scale": pool_scale[:, None],
    }
    p_prompt = p_prompt.reshape(depth, bp * tp, -1)
    p_sample = p_sample.reshape(depth, bs * ts, -1)
    pool_prev_s = jnp.pad(state_pool, ((0, 0), (0, 0), (POOL_HALO - POOL_STATE, 0), (0, 0)))
    conv_prev_s = halves(jnp.pad(state_conv, ((0, 0), (0, 0), (CONV_HALO - (CONV_WIDTH - 1), 0), (0, 0))))
    pool_prev_p = jnp.zeros((bp, POOL_HALO, POOL_WIDTH), F32)
    conv_prev_p = jnp.zeros((bp, CONV_HALO, ff_pad), F32)

    xp_, xs_ = x_prompt, x_sample
    outs = {k: [] for k in ("ks", "vs", "poolp", "pools", "convp", "convs", "gvs")}
    kv_prompt = [jnp.zeros((depth, bp * tp * SB_HEADS, HEAD_DIM), F32) for _ in range(2)]
    for l in range(depth):
        final_nw = final_norm_w[None] if l == depth - 1 else None
        attn_p = functools.partial(_attn_prompt, sb_bias=sb_bias, layer=l, blk=_tile(tp, ATTN_BLOCK))
        xp_, kv_prompt, rest_p, csa_p, csg_p = _layer(
            xp_, p_prompt, w, l, pool_prev_p, conv_prev_p, conv_prev_p, 0, attn_p, kv_stack=kv_prompt,
            final_nw=final_nw)
        nbias_col = jnp.repeat(sb_bias[l], ts)[:, None] * (-LOG2E)

        def attn_s(qkv4):
            return _attn_sample(qkv4[0], qkv4[1], qkv4[2], nbias_col, cache_k, cache_v, page_table, l,
                                n_pp=_tile(n_pages, SAMPLE_PAGES_PER_STEP))

        xs_, (kv_s,), rest_s, csa_s, csg_s = _layer(
            xs_, p_sample, w, l, pool_prev_s[l], conv_prev_s[l, :, :, 0], conv_prev_s[l, :, :, 1], past, attn_s,
            final_nw=final_nw)

        outs["ks"].append(kv_s[0].reshape(bs, ts, SB_HEADS, HEAD_DIM))
        outs["vs"].append(kv_s[1].reshape(bs, ts, SB_HEADS, HEAD_DIM))
        xpool_p = rest_p.reshape(bp, tp, REST_WIDTH)[:, :, 2 * GMLP_WIDTH:]
        xpool_s = rest_s.reshape(bs, ts, REST_WIDTH)[:, :, 2 * GMLP_WIDTH:]
        outs["poolp"].append(xpool_p[:, -POOL_STATE:])
        outs["pools"].append(jnp.concatenate([state_pool[l], xpool_s], axis=1)[:, -POOL_STATE:])
        tail = slice(CONV_HALO - (CONV_WIDTH - 1), CONV_HALO)
        outs["convp"].append(jnp.concatenate([csa_p[:, -1, tail, :d_ff], csg_p[:, -1, tail, :d_ff]], axis=-1))
        outs["convs"].append(jnp.concatenate([csa_s[:, -1, tail, :d_ff], csg_s[:, -1, tail, :d_ff]], axis=-1))
        outs["gvs"].append(rest_s.reshape(bs, ts, REST_WIDTH)[:, :, GMLP_WIDTH:2 * GMLP_WIDTH])

    k_prompt, v_prompt = (a.reshape(depth, bp, tp, SB_HEADS, HEAD_DIM) for a in kv_prompt)
    return (xp_, xs_,
            k_prompt, v_prompt, jnp.stack(outs["ks"]), jnp.stack(outs["vs"]),
            jnp.stack(outs["poolp"]), jnp.stack(outs["pools"]),
            jnp.stack(outs["convp"]), jnp.stack(outs["convs"]), jnp.stack(outs["gvs"]))
```
